```python
import numpy as np
import jax
import jax.numpy as jnp
from jax import lax

D_MODEL = 1024
BATCH = 8
SEQ = 2048
DEPTH = 4

GRID_W = 64
CTX_LEN = 256
N_MIXERS = 3
NORM_EPS = 1e-6
ROPE_THETA = 10000.0
NEG_INF = -1e30

A_HEADS = 16
A_KV_HEADS = 4
A_GROUP = A_HEADS // A_KV_HEADS
A_HEAD_DIM = D_MODEL // A_HEADS
A_WINDOW = 128
A_BLOCK = 128

B_HEADS = 16
B_HEAD_DIM = D_MODEL // B_HEADS
NA_ROWS = 8
NA_COLS = 16
NA_QCOLS = 16
NA_KCOLS = NA_QCOLS + NA_COLS
NA_NCB = GRID_W // NA_QCOLS

C_HEADS = 16
C_NOPE_DIM = 64
C_ROPE_DIM = 32
C_V_DIM = 64
C_Q_LORA = 256
C_KV_LORA = 128
C_BLOCK = 128

N_EXPERTS = 32
TOP_K = 4
D_EXPERT = D_MODEL
SWIGLU_LIMIT = 7.0
SWIGLU_ALPHA = 1.702
MOE_BLOCK = 256

kernel_name = "hybrid_diffusion_swa_natten_mla_moe"

F32 = jnp.float32


def rmsnorm(x, g):
    xf = x.astype(F32)
    y = xf * lax.rsqrt(jnp.mean(xf * xf, axis=-1, keepdims=True) + NORM_EPS)
    return (y * g.astype(F32)).astype(x.dtype)


def modulate(x, shift, scale):
    return x * (1 + scale) + shift


def axial_rope_angles(n_tokens, rot_dim):
    t = jnp.arange(n_tokens)
    row = (t // GRID_W).astype(F32)
    col = (t % GRID_W).astype(F32)
    n_freq = rot_dim // 4
    inv = ROPE_THETA ** (-jnp.arange(n_freq, dtype=F32) / n_freq)
    ang = jnp.concatenate([row[:, None] * inv, col[:, None] * inv], axis=-1)
    return jnp.cos(ang), jnp.sin(ang)


def apply_rope(x, cos, sin):
    shape = (1, cos.shape[0]) + (1,) * (x.ndim - 3) + (cos.shape[1],)
    cs = cos.reshape(shape)
    sn = sin.reshape(shape)
    xf = x.astype(F32).reshape(x.shape[:-1] + (-1, 2))
    x1, x2 = xf[..., 0], xf[..., 1]
    out = jnp.stack([x1 * cs - x2 * sn, x1 * sn + x2 * cs], axis=-1).reshape(x.shape)
    return out.astype(x.dtype)


def joint_softmax(parts, sink=None):
    s = jnp.concatenate([p.astype(F32) for p in parts], axis=-1)
    m = jnp.max(s, axis=-1, keepdims=True)
    if sink is not None:
        m = jnp.maximum(m, sink)
    e = jnp.exp(s - m)
    den = jnp.sum(e, axis=-1, keepdims=True)
    if sink is not None:
        den = den + jnp.exp(sink - m)
    p = e / den
    cuts = [int(v) for v in np.cumsum([q.shape[-1] for q in parts])[:-1]]
    return jnp.split(p, cuts, axis=-1)


def window_gqa_mixer(h, hc, w_in, w_out, sink, cos, sin, need_ctx):
    B, S, _ = h.shape
    C = hc.shape[1]
    qd = A_HEADS * A_HEAD_DIM
    kd = A_KV_HEADS * A_HEAD_DIM
    y = h @ w_in
    q = y[..., :qd].reshape(B, S, A_KV_HEADS, A_GROUP, A_HEAD_DIM)
    k = y[..., qd:qd + kd].reshape(B, S, A_KV_HEADS, A_HEAD_DIM)
    v = y[..., qd + kd:].reshape(B, S, A_KV_HEADS, A_HEAD_DIM)
    q = apply_rope(q, cos, sin)
    k = apply_rope(k, cos, sin)
    yc = hc @ (w_in if need_ctx else w_in[:, qd:])
    kc = yc[..., -2 * kd:-kd].reshape(B, C, A_KV_HEADS, A_HEAD_DIM)
    vc = yc[..., -kd:].reshape(B, C, A_KV_HEADS, A_HEAD_DIM)
    scale = A_HEAD_DIM ** -0.5
    sink_f = sink.astype(F32).reshape(A_KV_HEADS, A_GROUP)[None, :, :, None, None]
    nb = S // A_BLOCK

    def windows(t):
        tp = jnp.pad(t, ((0, 0), (A_BLOCK, A_BLOCK), (0, 0), (0, 0)))
        tp = tp.reshape(B, nb + 2, A_BLOCK, A_KV_HEADS, A_HEAD_DIM)
        w = jnp.concatenate([tp[:, :-2], tp[:, 1:-1], tp[:, 2:]], axis=2)
        return jnp.moveaxis(w, 1, 0)

    qb = jnp.moveaxis(q.reshape(B, nb, A_BLOCK, A_KV_HEADS, A_GROUP, A_HEAD_DIM), 1, 0)
    rel = jnp.arange(3 * A_BLOCK)[None, :] - A_BLOCK - jnp.arange(A_BLOCK)[:, None]
    band = jnp.abs(rel) <= A_WINDOW

    def block(args):
        qi, ki, vi, b = args
        kpos = (b - 1) * A_BLOCK + jnp.arange(3 * A_BLOCK)
        valid = band & ((kpos >= 0) & (kpos < S))[None, :]
        s_loc = jnp.einsum('bqkgd,bjkd->bkgqj', qi, ki).astype(F32) * scale
        s_loc = jnp.where(valid, s_loc, NEG_INF)
        s_ctx = jnp.einsum('bqkgd,bckd->bkgqc', qi, kc).astype(F32) * scale
        p_loc, p_ctx = joint_softmax([s_loc, s_ctx], sink_f)
        return (jnp.einsum('bkgqj,bjkd->bqkgd', p_loc.astype(vi.dtype), vi)
                + jnp.einsum('bkgqc,bckd->bqkgd', p_ctx.astype(vc.dtype), vc))

    o = lax.map(block, (qb, windows(k), windows(v), jnp.arange(nb)))
    out = jnp.moveaxis(o, 0, 1).reshape(B, S, qd) @ w_out
    out_c = None
    if need_ctx:
        qc = yc[..., :qd].reshape(B, C, A_KV_HEADS, A_GROUP, A_HEAD_DIM)
        s = jnp.einsum('bqkgd,bckd->bkgqc', qc, kc).astype(F32) * scale
        (p,) = joint_softmax([s], sink_f)
        oc = jnp.einsum('bkgqc,bckd->bqkgd', p.astype(vc.dtype), vc)
        out_c = oc.reshape(B, C, qd) @ w_out
    return out, out_c


def neighbourhood_mixer(h, hc, w_in, w_out, rpb, need_ctx):
    B, S, _ = h.shape
    C = hc.shape[1]
    rows = S // GRID_W
    kh = min(NA_ROWS, rows)
    wd = B_HEADS * B_HEAD_DIM
    y = h @ w_in
    q = y[..., :wd].reshape(B, S, B_HEADS, B_HEAD_DIM)
    k = y[..., wd:2 * wd].reshape(B, S, B_HEADS, B_HEAD_DIM)
    v = y[..., 2 * wd:].reshape(B, S, B_HEADS, B_HEAD_DIM)
    yc = hc @ (w_in if need_ctx else w_in[:, wd:])
    kc = yc[..., -2 * wd:-wd].reshape(B, C, B_HEADS, B_HEAD_DIM)
    vc = yc[..., -wd:].reshape(B, C, B_HEADS, B_HEAD_DIM)
    scale = B_HEAD_DIM ** -0.5

    qcol = np.arange(NA_NCB)[:, None] * NA_QCOLS + np.arange(NA_QCOLS)[None, :]
    cstart = np.clip(np.arange(NA_NCB) * NA_QCOLS - NA_COLS // 2, 0, GRID_W - NA_KCOLS)
    kcol = cstart[:, None] + np.arange(NA_KCOLS)[None, :]
    wstart = np.clip(qcol - NA_COLS // 2, 0, GRID_W - NA_COLS)
    col_ok = ((kcol[:, None, :] >= wstart[..., None])
              & (kcol[:, None, :] < wstart[..., None] + NA_COLS))
    col_ok = jnp.asarray(col_ok[:, :, None, :])
    dc_idx = jnp.asarray(np.clip(kcol[:, None, :] - qcol[..., None] + NA_COLS - 1, 0, 2 * NA_COLS - 2))
    kcol_j = jnp.asarray(kcol)

    q_rows = jnp.moveaxis(q.reshape(B, rows, NA_NCB, NA_QCOLS, B_HEADS, B_HEAD_DIM), 1, 0)
    k_grid = k.reshape(B, rows, GRID_W, B_HEADS, B_HEAD_DIM)
    v_grid = v.reshape(B, rows, GRID_W, B_HEADS, B_HEAD_DIM)
    rpb_f = rpb.astype(F32)

    def row_block(args):
        qr, r = args
        r0 = jnp.clip(r - kh // 2, 0, rows - kh)
        kr = lax.dynamic_slice_in_dim(k_grid, r0, kh, axis=1)[:, :, kcol_j]
        vr = lax.dynamic_slice_in_dim(v_grid, r0, kh, axis=1)[:, :, kcol_j]
        s_loc = jnp.einsum('bjqhd,brjkhd->bhjqrk', qr, kr).astype(F32) * scale
        dr_idx = r0 + jnp.arange(kh) - r + NA_ROWS - 1
        bias = rpb_f[:, dr_idx[None, None, :, None], dc_idx[:, :, None, :]]
        s_loc = jnp.where(col_ok, s_loc + bias, NEG_INF)
        s_loc = s_loc.reshape(B, B_HEADS, NA_NCB, NA_QCOLS, kh * NA_KCOLS)
        s_ctx = jnp.einsum('bjqhd,bchd->bhjqc', qr, kc).astype(F32) * scale
        p_loc, p_ctx = joint_softmax([s_loc, s_ctx])
        p_loc = p_loc.reshape(B, B_HEADS, NA_NCB, NA_QCOLS, kh, NA_KCOLS)
        return (jnp.einsum('bhjqrk,brjkhd->bjqhd', p_loc.astype(vr.dtype), vr)
                + jnp.einsum('bhjqc,bchd->bjqhd', p_ctx.astype(vc.dtype), vc))

    o = lax.map(row_block, (q_rows, jnp.arange(rows)))
    out = jnp.moveaxis(o, 0, 1).reshape(B, S, wd) @ w_out
    out_c = None
    if need_ctx:
        qc = yc[..., :wd].reshape(B, C, B_HEADS, B_HEAD_DIM)
        s = jnp.einsum('bqhd,bchd->bhqc', qc, kc).astype(F32) * scale
        (p,) = joint_softmax([s])
        oc = jnp.einsum('bhqc,bchd->bqhd', p.astype(vc.dtype), vc)
        out_c = oc.reshape(B, C, wd) @ w_out
    return out, out_c


def mla_mixer(h, hc, w_in, q_norm, kv_norm, w_uq, w_ukv, w_out, cos, sin, need_ctx):
    B, S, _ = h.shape
    C = hc.shape[1]
    H = C_HEADS
    scale = (C_NOPE_DIM + C_ROPE_DIM) ** -0.5

    def kv_path(yk, n):
        ckv = rmsnorm(yk[..., :C_KV_LORA], kv_norm)
        k_rope = yk[..., C_KV_LORA:]
        kv = (ckv @ w_ukv).reshape(B, n, H, C_NOPE_DIM + C_V_DIM)
        return kv[..., :C_NOPE_DIM], k_rope, kv[..., C_NOPE_DIM:]

    def q_path(yq, n):
        cq = rmsnorm(yq, q_norm)
        qq = (cq @ w_uq).reshape(B, n, H, C_NOPE_DIM + C_ROPE_DIM)
        return qq[..., :C_NOPE_DIM], qq[..., C_NOPE_DIM:]

    def attend(qn, qr, kn, kr, vv):
        s = (jnp.einsum('bqhd,bkhd->bhqk', qn, kn)
             + jnp.einsum('bqhd,bkd->bhqk', qr, kr)).astype(F32) * scale
        (p,) = joint_softmax([s])
        return jnp.einsum('bhqk,bkhd->bqhd', p.astype(vv.dtype), vv)

    y = h @ w_in
    qn, qr = q_path(y[..., :C_Q_LORA], S)
    kn, kr, v = kv_path(y[..., C_Q_LORA:], S)
    qr = apply_rope(qr, cos, sin)
    kr = apply_rope(kr, cos, sin)
    yc = hc @ (w_in if need_ctx else w_in[:, C_Q_LORA:])
    kn_c, kr_c, v_c = kv_path(yc[..., -(C_KV_LORA + C_ROPE_DIM):], C)
    kn_all = jnp.concatenate([kn_c, kn], axis=1)
    kr_all = jnp.concatenate([kr_c, kr], axis=1)
    v_all = jnp.concatenate([v_c, v], axis=1)
    nb = S // C_BLOCK
    qn_b = jnp.moveaxis(qn.reshape(B, nb, C_BLOCK, H, C_NOPE_DIM), 1, 0)
    qr_b = jnp.moveaxis(qr.reshape(B, nb, C_BLOCK, H, C_ROPE_DIM), 1, 0)
    o = lax.map(lambda a: attend(a[0], a[1], kn_all, kr_all, v_all), (qn_b, qr_b))
    out = jnp.moveaxis(o, 0, 1).reshape(B, S, H * C_V_DIM) @ w_out
    out_c = None
    if need_ctx:
        qn_c, qr_c = q_path(yc[..., :C_Q_LORA], C)
        oc = attend(qn_c, qr_c, kn_c, kr_c, v_c)
        out_c = oc.reshape(B, C, H * C_V_DIM) @ w_out
    return out, out_c


def moe_ffn(t, router_w, router_b, w_gate, b_gate, w_up, b_up, w_down, b_down):
    n, d = t.shape
    logits = (t @ router_w).astype(F32) + router_b.astype(F32)
    top_val, top_idx = lax.top_k(logits, TOP_K)
    gates = jax.nn.softmax(top_val, axis=-1)
    nk = n * TOP_K
    flat_e = top_idx.reshape(-1).astype(jnp.int32)
    order = jnp.argsort(flat_e)
    sorted_e = flat_e[order]
    counts = jnp.bincount(flat_e, length=N_EXPERTS).astype(jnp.int32)
    padded = (counts + MOE_BLOCK - 1) // MOE_BLOCK * MOE_BLOCK
    pad_end = jnp.cumsum(padded)
    pad_start = pad_end - padded
    grp_start = jnp.cumsum(counts) - counts
    dest = pad_start[sorted_e] + jnp.arange(nk, dtype=jnp.int32) - grp_start[sorted_e]
    n_blocks = -(-nk // MOE_BLOCK) + N_EXPERTS
    slot_token = jnp.full((n_blocks * MOE_BLOCK,), n, jnp.int32).at[dest].set(
        (order // TOP_K).astype(jnp.int32))
    block_expert = jnp.minimum(
        jnp.searchsorted(pad_end, jnp.arange(n_blocks, dtype=jnp.int32) * MOE_BLOCK, side='right'),
        N_EXPERTS - 1)
    t_pad = jnp.concatenate([t, jnp.zeros((1, d), t.dtype)], axis=0)
    x_blocks = t_pad[slot_token].reshape(n_blocks, MOE_BLOCK, d)

    def expert_block(args):
        xb, e = args
        g = xb @ w_gate[e] + b_gate[e]
        u = xb @ w_up[e] + b_up[e]
        g = jnp.minimum(g, SWIGLU_LIMIT)
        u = jnp.clip(u, -SWIGLU_LIMIT, SWIGLU_LIMIT)
        a = g * jax.nn.sigmoid(SWIGLU_ALPHA * g) * (u + 1)
        return a @ w_down[e] + b_down[e]

    y_slots = lax.map(expert_block, (x_blocks, block_expert)).reshape(-1, d)
    slot_of = jnp.zeros((nk,), jnp.int32).at[order].set(dest)
    y = y_slots[slot_of].reshape(n, TOP_K, d)
    return jnp.einsum('nkd,nk->nd', y, gates.astype(y.dtype))


def setup_inputs(seed: int = 0) -> dict:
    key = jax.random.key(seed)
    keys = iter(jax.random.split(key, 32))

    def rnd(shape, scale):
        return jax.random.normal(next(keys), shape, jnp.float32) * scale

    D = D_MODEL
    n_a = len(range(0, DEPTH, N_MIXERS))
    n_b = len(range(1, DEPTH, N_MIXERS))
    n_c = len(range(2, DEPTH, N_MIXERS))
    a_qd = A_HEADS * A_HEAD_DIM
    a_in = a_qd + 2 * A_KV_HEADS * A_HEAD_DIM
    b_w = B_HEADS * B_HEAD_DIM
    c_in = C_Q_LORA + C_KV_LORA + C_ROPE_DIM
    return {
        "x": rnd((BATCH, SEQ, D), 1.0),
        "c": rnd((BATCH, D), 1.0),
        "ctx": rnd((BATCH, CTX_LEN, D), 1.0),
        "c_ctx": rnd((D,), 1.0),
        "ada_w": rnd((DEPTH, D, 6 * D), 0.5 * D ** -0.5),
        "ada_b": rnd((DEPTH, 6 * D), 0.02),
        "norm_mix": 1.0 + rnd((DEPTH, D), 0.05),
        "norm_ffn": 1.0 + rnd((DEPTH, D), 0.05),
        "norm_out": 1.0 + rnd((D,), 0.05),
        "a_w_in": rnd((n_a, D, a_in), D ** -0.5),
        "a_w_out": rnd((n_a, a_qd, D), a_qd ** -0.5),
        "a_sink": rnd((n_a, A_HEADS), 0.5),
        "b_w_in": rnd((n_b, D, 3 * b_w), D ** -0.5),
        "b_w_out": rnd((n_b, b_w, D), b_w ** -0.5),
        "b_rpb": rnd((n_b, B_HEADS, 2 * NA_ROWS - 1, 2 * NA_COLS - 1), 0.2),
        "c_w_in": rnd((n_c, D, c_in), D ** -0.5),
        "c_q_norm": 1.0 + rnd((n_c, C_Q_LORA), 0.05),
        "c_kv_norm": 1.0 + rnd((n_c, C_KV_LORA), 0.05),
        "c_w_uq": rnd((n_c, C_Q_LORA, C_HEADS * (C_NOPE_DIM + C_ROPE_DIM)), C_Q_LORA ** -0.5),
        "c_w_ukv": rnd((n_c, C_KV_LORA, C_HEADS * (C_NOPE_DIM + C_V_DIM)), C_KV_LORA ** -0.5),
        "c_w_out": rnd((n_c, C_HEADS * C_V_DIM, D), (C_HEADS * C_V_DIM) ** -0.5),
        "router_w": rnd((DEPTH, D, N_EXPERTS), D ** -0.5),
        "router_b": rnd((DEPTH, N_EXPERTS), 0.01),
        "moe_w_gate": rnd((DEPTH, N_EXPERTS, D, D_EXPERT), D ** -0.5),
        "moe_b_gate": rnd((DEPTH, N_EXPERTS, D_EXPERT), 0.02),
        "moe_w_up": rnd((DEPTH, N_EXPERTS, D, D_EXPERT), D ** -0.5),
        "moe_b_up": rnd((DEPTH, N_EXPERTS, D_EXPERT), 0.02),
        "moe_w_down": rnd((DEPTH, N_EXPERTS, D_EXPERT, D), D_EXPERT ** -0.5),
        "moe_b_down": rnd((DEPTH, N_EXPERTS, D), 0.02),
    }


def reference(x, c, ctx, c_ctx, ada_w, ada_b, norm_mix, norm_ffn, norm_out,
              a_w_in, a_w_out, a_sink, b_w_in, b_w_out, b_rpb,
              c_w_in, c_q_norm, c_kv_norm, c_w_uq, c_w_ukv, c_w_out,
              router_w, router_b, moe_w_gate, moe_b_gate, moe_w_up, moe_b_up,
              moe_w_down, moe_b_down):
    B, S, D = x.shape
    C = ctx.shape[1]
    cos_a, sin_a = axial_rope_angles(S, A_HEAD_DIM)
    cos_c, sin_c = axial_rope_angles(S, C_ROPE_DIM)
    silu_c = jax.nn.silu(c)
    silu_cc = jax.nn.silu(c_ctx)
    xl, xc = x, ctx
    for l in range(DEPTH):
        last = l == DEPTH - 1
        kind, slot = l % N_MIXERS, l // N_MIXERS
        mod = (silu_c @ ada_w[l] + ada_b[l])[:, None, :]
        sh1, sc1, g1, sh2, sc2, g2 = jnp.split(mod, 6, axis=-1)
        modc = silu_cc @ ada_w[l] + ada_b[l]
        ch1, cs1, cg1, ch2, cs2, cg2 = jnp.split(modc, 6, axis=-1)

        h = modulate(rmsnorm(xl, norm_mix[l]), sh1, sc1)
        hc = modulate(rmsnorm(xc, norm_mix[l]), ch1, cs1)
        if kind == 0:
            y, yc = window_gqa_mixer(h, hc, a_w_in[slot], a_w_out[slot], a_sink[slot],
                                     cos_a, sin_a, not last)
        elif kind == 1:
            y, yc = neighbourhood_mixer(h, hc, b_w_in[slot], b_w_out[slot], b_rpb[slot], not last)
        else:
            y, yc = mla_mixer(h, hc, c_w_in[slot], c_q_norm[slot], c_kv_norm[slot], c_w_uq[slot],
                              c_w_ukv[slot], c_w_out[slot], cos_c, sin_c, not last)
        xl = xl + g1 * y

        h = modulate(rmsnorm(xl, norm_ffn[l]), sh2, sc2)
        moe_args = (router_w[l], router_b[l], moe_w_gate[l], moe_b_gate[l], moe_w_up[l],
                    moe_b_up[l], moe_w_down[l], moe_b_down[l])
        if last:
            f = moe_ffn(h.reshape(B * S, D), *moe_args)
            xl = xl + g2 * f.reshape(B, S, D)
        else:
            xc = xc + cg1 * yc
            hc = modulate(rmsnorm(xc, norm_ffn[l]), ch2, cs2)
            tokens = jnp.concatenate([h.reshape(B * S, D), hc.reshape(B * C, D)], axis=0)
            f = moe_ffn(tokens, *moe_args)
            xl = xl + g2 * f[:B * S].reshape(B, S, D)
            xc = xc + cg2 * f[B * S:].reshape(B, C, D)
    return rmsnorm(xl, norm_out)
```

```python
import functools

import numpy as np
import jax
import jax.numpy as jnp
from jax import lax
from jax.experimental import pallas as pl
from jax.experimental.pallas import tpu as pltpu

F32 = jnp.float32
BF16 = jnp.bfloat16
I32 = jnp.int32

D_MODEL = 1024
DEPTH = 4
SEQ = 2048
GRID_W = 64
GRID_H = SEQ // GRID_W
CTX_LEN = 256
N_MIXERS = 3
NORM_EPS = 1e-6
ROPE_THETA = 10000.0
NEG_INF = -1e30

A_HEADS, A_KV_HEADS, A_HEAD_DIM, A_WINDOW = 16, 4, 64, 128
B_HEADS, B_HEAD_DIM, NA_ROWS, NA_COLS = 16, 64, 8, 16
C_HEADS, C_NOPE, C_ROPE, C_V, C_Q_LORA, C_KV_LORA = 16, 64, 32, 64, 256, 128
N_EXPERTS, TOP_K = 32, 4
SWIGLU_LIMIT, SWIGLU_ALPHA = 7.0, 1.702

LANES = 128
N_PAIRS = 8
MOD_ROWS = 16

TM_PROJ = 512
TM_POST = 256
TM_ROUTE = 256
TM_EXPERT = 512
TQ_GQA = 256
TQ_MLA = 256
NA_TILE_ROWS = 4
NA_SLAB_ROWS = 12
VMEM_MB = 56


def _cparams(sem, vmem_mb=VMEM_MB):
    return pltpu.CompilerParams(dimension_semantics=sem, vmem_limit_bytes=vmem_mb * 1024 * 1024)


def _dot(a, b):
    return jnp.dot(a, b, preferred_element_type=F32)


def _dot_nt(a, b):
    return lax.dot_general(a, b, (((1,), (1,)), ((), ())), preferred_element_type=F32)


def _split(x):
    hi = x.astype(BF16)
    lo = (x - hi.astype(F32)).astype(BF16)
    return hi, lo


def _norm_mod(x, g, shift, scale):
    ms = jnp.mean(x * x, axis=-1, keepdims=True)
    y = x * lax.rsqrt(ms + NORM_EPS) * g
    return y * (1.0 + scale) + shift


def _rmsnorm(x, g):
    ms = jnp.mean(x * x, axis=-1, keepdims=True)
    return x * lax.rsqrt(ms + NORM_EPS) * g


def _rope_block(xb, cos, sin_signed, even):
    nxt = pltpu.roll(xb, LANES - 1, 1)
    prv = pltpu.roll(xb, 1, 1)
    return xb * cos + jnp.where(even, nxt, prv) * sin_signed


def _softmax_pv(s_parts, v_parts, sink):
    m = s_parts[0].max(axis=-1, keepdims=True)
    for s in s_parts[1:]:
        m = jnp.maximum(m, s.max(axis=-1, keepdims=True))
    if sink is not None:
        m = jnp.maximum(m, sink)
    den = None
    acc = None
    for s, v in zip(s_parts, v_parts):
        e = jnp.exp(s - m)
        d = e.sum(axis=-1, keepdims=True)
        a = _dot(e.astype(BF16), v)
        den = d if den is None else den + d
        acc = a if acc is None else acc + a
    if sink is not None:
        den = den + jnp.exp(sink - m)
    return acc / den


def _two_heads(qs, parts, scale, sinks, lo):
    outs = []
    for half in range(2):
        s_list = []
        for k, _, post in parts:
            s = _dot_nt(qs[half], k) * scale
            if post is not None:
                s = post(s, half)
            s_list.append(s)
        sink = None if sinks is None else sinks[half]
        outs.append(_softmax_pv(s_list, [p[1] for p in parts], sink))
    return jnp.where(lo, outs[0], outs[1]).astype(BF16)


def _lane_masks(rows):
    lane = lax.broadcasted_iota(I32, (rows, LANES), 1)
    return lane, lane < (LANES // 2)


def _mask_halves(q, lo):
    z = jnp.zeros_like(q)
    return [jnp.where(lo, q, z), jnp.where(lo, z, q)]


def _mod_kernel(c_ref, w_ref, b_ref, o_ref):
    c = c_ref[...]
    s = c * jax.nn.sigmoid(c)
    s_hi, s_lo = _split(s)
    w_hi, w_lo = _split(w_ref[0])
    o_ref[0] = _dot(s_hi, w_hi) + _dot(s_lo, w_hi) + _dot(s_hi, w_lo) + b_ref[0]


def _modulation(cc, ada_w, ada_b):
    tn = 1536
    n6 = ada_w.shape[-1]
    return pl.pallas_call(
        _mod_kernel,
        grid=(DEPTH, n6 // tn),
        in_specs=[
            pl.BlockSpec((MOD_ROWS, D_MODEL), lambda l, j: (0, 0)),
            pl.BlockSpec((1, D_MODEL, tn), lambda l, j: (l, 0, j)),
            pl.BlockSpec((1, 1, tn), lambda l, j: (l, 0, j)),
        ],
        out_specs=pl.BlockSpec((1, MOD_ROWS, tn), lambda l, j: (l, 0, j)),
        out_shape=jax.ShapeDtypeStruct((DEPTH, MOD_ROWS, n6), F32),
        compiler_params=_cparams(("arbitrary", "arbitrary")),
        name="modulation",
    )(cc, ada_w, ada_b.reshape(DEPTH, 1, n6))


def _mod_spec(l, which, n_lat_tiles_per_batch, batch):
    def index(i):
        return (l, jnp.minimum(i // n_lat_tiles_per_batch, batch), which, 0, 0)
    return pl.BlockSpec((1, 1, 1, 1, D_MODEL), index)


def _proj_kernel(x_ref, g_ref, sh_ref, sc_ref, w_ref, *rest, n_rope):
    if n_rope:
        cos_ref, sin_ref, o_ref = rest
    else:
        (o_ref,) = rest
    h = _norm_mod(x_ref[...], g_ref[0], sh_ref[0, 0, 0], sc_ref[0, 0, 0])
    y = _dot(h.astype(BF16), w_ref[...])
    if n_rope:
        cos = cos_ref[...]
        sin = sin_ref[...]
        lane, _ = _lane_masks(x_ref.shape[0])
        even = (lane & 1) == 0
    for j in range(o_ref.shape[0]):
        yb = y[:, j * LANES:(j + 1) * LANES]
        if j < n_rope:
            yb = _rope_block(yb, cos, sin, even)
        o_ref[j] = yb.astype(BF16)


def _project(x, norm_g, mods, l, w, rope, n_rope, batch):
    n_rows = x.shape[0]
    tm = TM_PROJ
    n_tiles = n_rows // tm
    tiles_per_batch = SEQ // tm
    n_lat_tiles = batch * tiles_per_batch
    ncb = w.shape[1] // LANES
    in_specs = [
        pl.BlockSpec((tm, D_MODEL), lambda i: (i, 0)),
        pl.BlockSpec((1, 1, D_MODEL), lambda i: (l, 0, 0)),
        _mod_spec(l, 0, tiles_per_batch, batch),
        _mod_spec(l, 1, tiles_per_batch, batch),
        pl.BlockSpec(w.shape, lambda i: (0, 0)),
    ]
    args = [x, norm_g, mods, mods, w]
    if n_rope:
        def rope_index(i):
            return (jnp.where(i < n_lat_tiles, i % tiles_per_batch, tiles_per_batch), 0)
        in_specs += [pl.BlockSpec((tm, LANES), rope_index)] * 2
        args += list(rope)
    return pl.pallas_call(
        functools.partial(_proj_kernel, n_rope=n_rope),
        grid=(n_tiles,),
        in_specs=in_specs,
        out_specs=pl.BlockSpec((ncb, tm, LANES), lambda i: (0, i, 0)),
        out_shape=jax.ShapeDtypeStruct((ncb, n_rows, LANES), BF16),
        compiler_params=_cparams(("arbitrary",)),
        name=f"project_l{l}",
    )(*args)


def _mla_proj_kernel(x_ref, g_ref, sh_ref, sc_ref, w_ref, qn_ref, kvn_ref, wq_ref, wkv_ref,
                     cos_ref, sin_ref, o_ref):
    h = _norm_mod(x_ref[...], g_ref[0], sh_ref[0, 0, 0], sc_ref[0, 0, 0])
    y = _dot(h.astype(BF16), w_ref[...])
    cq = _rmsnorm(y[:, :C_Q_LORA], qn_ref[...])
    qq = _dot(cq.astype(BF16), wq_ref[...])
    ckv = _rmsnorm(y[:, C_Q_LORA:C_Q_LORA + C_KV_LORA], kvn_ref[...])
    kv = _dot(ckv.astype(BF16), wkv_ref[...])
    kr = y[:, C_Q_LORA + C_KV_LORA:]
    kr = kr + pltpu.roll(kr, 32, 1) + pltpu.roll(kr, 64, 1) + pltpu.roll(kr, 96, 1)
    cos = cos_ref[...]
    sin = sin_ref[...]
    lane, _ = _lane_masks(x_ref.shape[0])
    even = (lane & 1) == 0
    for j in range(N_PAIRS):
        o_ref[j] = qq[:, j * LANES:(j + 1) * LANES].astype(BF16)
    for j in range(2 * N_PAIRS):
        o_ref[N_PAIRS + j] = kv[:, j * LANES:(j + 1) * LANES].astype(BF16)
    for j in range(4):
        qb = qq[:, (N_PAIRS + j) * LANES:(N_PAIRS + j + 1) * LANES]
        o_ref[3 * N_PAIRS + j] = _rope_block(qb, cos, sin, even).astype(BF16)
    o_ref[3 * N_PAIRS + 4] = _rope_block(kr, cos, sin, even).astype(BF16)


def _mla_project(x, norm_g, mods, l, w_in, q_norm, kv_norm, w_uq, w_ukv, rope, batch):
    n_rows = x.shape[0]
    tm = TM_PROJ
    tiles_per_batch = SEQ // tm
    n_lat_tiles = batch * tiles_per_batch
    ncb = 3 * N_PAIRS + 5

    def rope_index(i):
        return (jnp.where(i < n_lat_tiles, i % tiles_per_batch, tiles_per_batch), 0)

    def full(a):
        return pl.BlockSpec(a.shape, lambda i: (0,) * a.ndim)

    return pl.pallas_call(
        _mla_proj_kernel,
        grid=(n_rows // tm,),
        in_specs=[
            pl.BlockSpec((tm, D_MODEL), lambda i: (i, 0)),
            pl.BlockSpec((1, 1, D_MODEL), lambda i: (l, 0, 0)),
            _mod_spec(l, 0, tiles_per_batch, batch),
            _mod_spec(l, 1, tiles_per_batch, batch),
            full(w_in), full(q_norm), full(kv_norm), full(w_uq), full(w_ukv),
            pl.BlockSpec((tm, LANES), rope_index),
            pl.BlockSpec((tm, LANES), rope_index),
        ],
        out_specs=pl.BlockSpec((ncb, tm, LANES), lambda i: (0, i, 0)),
        out_shape=jax.ShapeDtypeStruct((ncb, n_rows, LANES), BF16),
        compiler_params=_cparams(("arbitrary",)),
        name=f"mla_project_l{l}",
    )(x, norm_g, mods, mods, w_in, q_norm, kv_norm, w_uq, w_ukv, *rope)


def _gqa_kernel(sink_ref, q_ref, k_ref, v_ref, kc_ref, vc_ref, o_ref, *, tq):
    j = pl.program_id(1)
    ks = tq + 2 * A_WINDOW
    s0 = pl.multiple_of(jnp.clip(j * tq - A_WINDOW, 0, SEQ - ks), LANES)
    qpos = j * tq + lax.broadcasted_iota(I32, (tq, ks), 0)
    kpos = s0 + lax.broadcasted_iota(I32, (tq, ks), 1)
    band = jnp.abs(kpos - qpos) <= A_WINDOW
    _, lo = _lane_masks(tq)
    scale = A_HEAD_DIM ** -0.5

    def masked(s, half):
        return jnp.where(band, s, NEG_INF)

    def pair(c, carry):
        kh = c // 2
        k = k_ref[kh, pl.ds(s0, ks), :]
        v = v_ref[kh, pl.ds(s0, ks), :]
        parts = [(k, v, masked), (kc_ref[kh], vc_ref[kh], None)]
        sinks = [sink_ref[2 * c], sink_ref[2 * c + 1]]
        o_ref[c] = _two_heads(_mask_halves(q_ref[c], lo), parts, scale, sinks, lo)
        return carry

    lax.fori_loop(0, N_PAIRS, pair, 0)


def _gqa_attention(y, sink, batch):
    tq = TQ_GQA
    n_lat = batch * SEQ
    ctx0 = n_lat // CTX_LEN
    kvb = A_KV_HEADS
    return pl.pallas_call(
        functools.partial(_gqa_kernel, tq=tq),
        grid=(batch, SEQ // tq),
        in_specs=[
            pl.BlockSpec(memory_space=pltpu.SMEM),
            pl.BlockSpec((N_PAIRS, tq, LANES), lambda b, j: (0, b * (SEQ // tq) + j, 0)),
            pl.BlockSpec((kvb, SEQ, LANES), lambda b, j: (2, b, 0)),
            pl.BlockSpec((kvb, SEQ, LANES), lambda b, j: (3, b, 0)),
            pl.BlockSpec((kvb, CTX_LEN, LANES), lambda b, j: (2, ctx0 + b, 0)),
            pl.BlockSpec((kvb, CTX_LEN, LANES), lambda b, j: (3, ctx0 + b, 0)),
        ],
        out_specs=pl.BlockSpec((N_PAIRS, tq, LANES), lambda b, j: (0, b * (SEQ // tq) + j, 0)),
        out_shape=jax.ShapeDtypeStruct((N_PAIRS, n_lat, LANES), BF16),
        compiler_params=_cparams(("arbitrary", "arbitrary")),
        name="gqa_attention",
    )(sink, y, y, y, y, y)


def _gqa_ctx_kernel(sink_ref, q_ref, k_ref, v_ref, o_ref):
    _, lo = _lane_masks(CTX_LEN)

    def pair(c, carry):
        kh = c // 2
        sinks = [sink_ref[2 * c], sink_ref[2 * c + 1]]
        o_ref[c] = _two_heads(_mask_halves(q_ref[c], lo), [(k_ref[kh], v_ref[kh], None)],
                              A_HEAD_DIM ** -0.5, sinks, lo)
        return carry

    lax.fori_loop(0, N_PAIRS, pair, 0)


def _gqa_ctx_attention(y, sink, batch):
    ctx0 = batch * SEQ // CTX_LEN
    kvb = A_KV_HEADS
    return pl.pallas_call(
        _gqa_ctx_kernel,
        grid=(batch,),
        in_specs=[
            pl.BlockSpec(memory_space=pltpu.SMEM),
            pl.BlockSpec((N_PAIRS, CTX_LEN, LANES), lambda b: (0, ctx0 + b, 0)),
            pl.BlockSpec((kvb, CTX_LEN, LANES), lambda b: (2, ctx0 + b, 0)),
            pl.BlockSpec((kvb, CTX_LEN, LANES), lambda b: (3, ctx0 + b, 0)),
        ],
        out_specs=pl.BlockSpec((N_PAIRS, CTX_LEN, LANES), lambda b: (0, b, 0)),
        out_shape=jax.ShapeDtypeStruct((N_PAIRS, batch * CTX_LEN, LANES), BF16),
        compiler_params=_cparams(("arbitrary",)),
        name="gqa_ctx_attention",
    )(sink, y, y, y)


def _na_kernel(q_ref, k_ref, v_ref, kc_ref, vc_ref, bias_ref, o_ref, *, tq, ks):
    t = pl.program_id(1)
    s0row = jnp.clip(NA_TILE_ROWS * t - NA_ROWS // 2, 0, GRID_H - NA_SLAB_ROWS)
    s0 = pl.multiple_of(s0row * GRID_W, GRID_W)
    _, lo = _lane_masks(tq)

    def pair(c, carry):
        k = k_ref[c, pl.ds(s0, ks), :]
        v = v_ref[c, pl.ds(s0, ks), :]

        def biased(s, half):
            return s + bias_ref[2 * c + half, 0]

        parts = [(k, v, biased), (kc_ref[c], vc_ref[c], None)]
        o_ref[c] = _two_heads(_mask_halves(q_ref[c], lo), parts, B_HEAD_DIM ** -0.5, None, lo)
        return carry

    lax.fori_loop(0, N_PAIRS, pair, 0)


def _na_attention(y, bias, batch):
    tq = NA_TILE_ROWS * GRID_W
    ks = NA_SLAB_ROWS * GRID_W
    n_t = SEQ // tq
    n_lat = batch * SEQ
    ctx0 = n_lat // CTX_LEN

    def bias_index(b, t):
        return (0, jnp.where(t == 0, 0, jnp.where(t == n_t - 1, 2, 1)), 0, 0)

    return pl.pallas_call(
        functools.partial(_na_kernel, tq=tq, ks=ks),
        grid=(batch, n_t),
        in_specs=[
            pl.BlockSpec((N_PAIRS, tq, LANES), lambda b, t: (0, b * n_t + t, 0)),
            pl.BlockSpec((N_PAIRS, SEQ, LANES), lambda b, t: (1, b, 0)),
            pl.BlockSpec((N_PAIRS, SEQ, LANES), lambda b, t: (2, b, 0)),
            pl.BlockSpec((N_PAIRS, CTX_LEN, LANES), lambda b, t: (1, ctx0 + b, 0)),
            pl.BlockSpec((N_PAIRS, CTX_LEN, LANES), lambda b, t: (2, ctx0 + b, 0)),
            pl.BlockSpec((B_HEADS, 1, tq, ks), bias_index),
        ],
        out_specs=pl.BlockSpec((N_PAIRS, tq, LANES), lambda b, t: (0, b * n_t + t, 0)),
        out_shape=jax.ShapeDtypeStruct((N_PAIRS, n_lat, LANES), BF16),
        compiler_params=_cparams(("arbitrary", "arbitrary"), 60),
        name="na_attention",
    )(y, y, y, y, y, bias)


def _na_ctx_kernel(q_ref, k_ref, v_ref, o_ref):
    _, lo = _lane_masks(CTX_LEN)

    def pair(c, carry):
        o_ref[c] = _two_heads(_mask_halves(q_ref[c], lo), [(k_ref[c], v_ref[c], None)],
                              B_HEAD_DIM ** -0.5, None, lo)
        return carry

    lax.fori_loop(0, N_PAIRS, pair, 0)


def _na_ctx_attention(y, batch):
    ctx0 = batch * SEQ // CTX_LEN
    return pl.pallas_call(
        _na_ctx_kernel,
        grid=(batch,),
        in_specs=[
            pl.BlockSpec((N_PAIRS, CTX_LEN, LANES), lambda b: (0, ctx0 + b, 0)),
            pl.BlockSpec((N_PAIRS, CTX_LEN, LANES), lambda b: (1, ctx0 + b, 0)),
            pl.BlockSpec((N_PAIRS, CTX_LEN, LANES), lambda b: (2, ctx0 + b, 0)),
        ],
        out_specs=pl.BlockSpec((N_PAIRS, CTX_LEN, LANES), lambda b: (0, b, 0)),
        out_shape=jax.ShapeDtypeStruct((N_PAIRS, batch * CTX_LEN, LANES), BF16),
        compiler_params=_cparams(("arbitrary",)),
        name="na_ctx_attention",
    )(y, y, y)


def _mla_queries(qn, qr, c, lane, lo):
    zero = jnp.zeros_like(qn)
    quarter = lane // C_ROPE
    qs = []
    for half in range(2):
        u = 2 * (c % 2) + half
        qn_m = jnp.where(lo, qn, zero) if half == 0 else jnp.where(lo, zero, qn)
        qr_m = jnp.where(quarter == u, qr, zero)
        qs.append(jnp.concatenate([qn_m, qr_m], axis=1))
    return qs


def _mla_kernel(qn_ref, qr_ref, kn_ref, v_ref, kr_ref, knc_ref, vc_ref, krc_ref, o_ref, *, tq):
    lane, lo = _lane_masks(tq)
    scale = (C_NOPE + C_ROPE) ** -0.5

    def pair(c, carry):
        kcat = jnp.concatenate([kn_ref[c], kr_ref[0]], axis=1)
        kccat = jnp.concatenate([knc_ref[c], krc_ref[0]], axis=1)
        parts = [(kccat, vc_ref[c], None), (kcat, v_ref[c], None)]
        qs = _mla_queries(qn_ref[c], qr_ref[c // 2], c, lane, lo)
        o_ref[c] = _two_heads(qs, parts, scale, None, lo)
        return carry

    lax.fori_loop(0, N_PAIRS, pair, 0)


def _mla_attention(y, batch):
    tq = TQ_MLA
    n_t = SEQ // tq
    n_lat = batch * SEQ
    ctx0 = n_lat // CTX_LEN
    kr_blk = 3 * N_PAIRS + 4
    return pl.pallas_call(
        functools.partial(_mla_kernel, tq=tq),
        grid=(batch, n_t),
        in_specs=[
            pl.BlockSpec((N_PAIRS, tq, LANES), lambda b, t: (0, b * n_t + t, 0)),
            pl.BlockSpec((4, tq, LANES), lambda b, t: (6, b * n_t + t, 0)),
            pl.BlockSpec((N_PAIRS, SEQ, LANES), lambda b, t: (1, b, 0)),
            pl.BlockSpec((N_PAIRS, SEQ, LANES), lambda b, t: (2, b, 0)),
            pl.BlockSpec((1, SEQ, LANES), lambda b, t: (kr_blk, b, 0)),
            pl.BlockSpec((N_PAIRS, CTX_LEN, LANES), lambda b, t: (1, ctx0 + b, 0)),
            pl.BlockSpec((N_PAIRS, CTX_LEN, LANES), lambda b, t: (2, ctx0 + b, 0)),
            pl.BlockSpec((1, CTX_LEN, LANES), lambda b, t: (kr_blk, ctx0 + b, 0)),
        ],
        out_specs=pl.BlockSpec((N_PAIRS, tq, LANES), lambda b, t: (0, b * n_t + t, 0)),
        out_shape=jax.ShapeDtypeStruct((N_PAIRS, n_lat, LANES), BF16),
        compiler_params=_cparams(("arbitrary", "arbitrary")),
        name="mla_attention",
    )(y, y, y, y, y, y, y, y)


def _mla_ctx_kernel(qn_ref, qr_ref, kn_ref, v_ref, kr_ref, o_ref):
    lane, lo = _lane_masks(CTX_LEN)

    def pair(c, carry):
        kcat = jnp.concatenate([kn_ref[c], kr_ref[0]], axis=1)
        qs = _mla_queries(qn_ref[c], qr_ref[c // 2], c, lane, lo)
        o_ref[c] = _two_heads(qs, [(kcat, v_ref[c], None)], (C_NOPE + C_ROPE) ** -0.5, None, lo)
        return carry

    lax.fori_loop(0, N_PAIRS, pair, 0)


def _mla_ctx_attention(y, batch):
    ctx0 = batch * SEQ // CTX_LEN
    kr_blk = 3 * N_PAIRS + 4
    return pl.pallas_call(
        _mla_ctx_kernel,
        grid=(batch,),
        in_specs=[
            pl.BlockSpec((N_PAIRS, CTX_LEN, LANES), lambda b: (0, ctx0 + b, 0)),
            pl.BlockSpec((4, CTX_LEN, LANES), lambda b: (6, ctx0 + b, 0)),
            pl.BlockSpec((N_PAIRS, CTX_LEN, LANES), lambda b: (1, ctx0 + b, 0)),
            pl.BlockSpec((N_PAIRS, CTX_LEN, LANES), lambda b: (2, ctx0 + b, 0)),
            pl.BlockSpec((1, CTX_LEN, LANES), lambda b: (kr_blk, ctx0 + b, 0)),
        ],
        out_specs=pl.BlockSpec((N_PAIRS, CTX_LEN, LANES), lambda b: (0, b, 0)),
        out_shape=jax.ShapeDtypeStruct((N_PAIRS, batch * CTX_LEN, LANES), BF16),
        compiler_params=_cparams(("arbitrary",)),
        name="mla_ctx_attention",
    )(y, y, y, y, y)


def _post_kernel(o_ref, w_ref, x_ref, g1_ref, gn_ref, sh_ref, sc_ref, rw_ref, rb_ref,
                 xo_ref, h_ref, idx_ref, gate_ref, rank_ref, cnt_ref, run_ref):
    tm = x_ref.shape[0]

    @pl.when(pl.program_id(0) == 0)
    def _():
        run_ref[...] = jnp.zeros_like(run_ref)

    o = jnp.concatenate([o_ref[c] for c in range(N_PAIRS)], axis=1)
    x = x_ref[...] + g1_ref[0, 0, 0] * _dot(o, w_ref[...])
    xo_ref[...] = x
    h = _norm_mod(x, gn_ref[0], sh_ref[0, 0, 0], sc_ref[0, 0, 0])
    h_ref[...] = h

    h_hi, h_lo = _split(h)
    rw_hi, rw_lo = _split(rw_ref[0])
    logits = _dot_nt(rw_hi, h_hi) + _dot_nt(rw_lo, h_hi) + _dot_nt(rw_hi, h_lo) + rb_ref[0]

    ie = lax.broadcasted_iota(I32, (N_EXPERTS, tm), 0).astype(F32)
    work = logits
    vals, sels = [], []
    for k in range(TOP_K):
        m = work.max(axis=0, keepdims=True)
        idx = jnp.min(jnp.where(work == m, ie, float(N_EXPERTS)), axis=0, keepdims=True)
        sel = ie == idx
        idx_ref[k:k + 1, :] = idx.astype(I32)
        vals.append(m)
        sels.append(sel)
        work = jnp.where(sel, -3.0e38, work)
    es = [jnp.exp(v - vals[0]) for v in vals]
    den = es[0] + es[1] + es[2] + es[3]
    for k in range(TOP_K):
        gate_ref[k:k + 1, :] = es[k] / den

    onehot = sels[0].astype(F32) + sels[1].astype(F32) + sels[2].astype(F32) + sels[3].astype(F32)
    before = lax.broadcasted_iota(I32, (tm, tm), 0) < lax.broadcasted_iota(I32, (tm, tm), 1)
    prefix = _dot(onehot.astype(BF16), before.astype(F32).astype(BF16)) + run_ref[...]
    for k in range(TOP_K):
        rank = jnp.sum(jnp.where(sels[k], prefix, 0.0), axis=0, keepdims=True)
        rank_ref[k:k + 1, :] = rank.astype(I32)
    run_ref[...] = run_ref[...] + onehot.sum(axis=1, keepdims=True)
    cnt_ref[...] = jnp.broadcast_to(run_ref[...], cnt_ref.shape)


def _post_attention(o, w_out, x, mods, l, norm_g, router_wt, router_b, n_rows, batch):
    tm = TM_POST
    n_tiles = n_rows // tm
    tiles_per_batch = SEQ // tm
    row_spec = pl.BlockSpec((tm, D_MODEL), lambda i: (i, 0))
    tok_spec = pl.BlockSpec((TOP_K, tm), lambda i: (0, i))
    return pl.pallas_call(
        _post_kernel,
        grid=(n_tiles,),
        in_specs=[
            pl.BlockSpec((N_PAIRS, tm, LANES), lambda i: (0, i, 0)),
            pl.BlockSpec(w_out.shape, lambda i: (0, 0)),
            row_spec,
            _mod_spec(l, 2, tiles_per_batch, batch),
            pl.BlockSpec((1, 1, D_MODEL), lambda i: (l, 0, 0)),
            _mod_spec(l, 3, tiles_per_batch, batch),
            _mod_spec(l, 4, tiles_per_batch, batch),
            pl.BlockSpec((1, N_EXPERTS, D_MODEL), lambda i: (l, 0, 0)),
            pl.BlockSpec((1, N_EXPERTS, 1), lambda i: (l, 0, 0)),
        ],
        out_specs=[row_spec, row_spec, tok_spec, tok_spec, tok_spec,
                   pl.BlockSpec((N_EXPERTS, LANES), lambda i: (0, 0))],
        out_shape=[
            jax.ShapeDtypeStruct((n_rows, D_MODEL), F32),
            jax.ShapeDtypeStruct((n_rows, D_MODEL), F32),
            jax.ShapeDtypeStruct((TOP_K, n_rows), I32),
            jax.ShapeDtypeStruct((TOP_K, n_rows), F32),
            jax.ShapeDtypeStruct((TOP_K, n_rows), I32),
            jax.ShapeDtypeStruct((N_EXPERTS, LANES), F32),
        ],
        scratch_shapes=[pltpu.VMEM((N_EXPERTS, 1), F32)],
        compiler_params=_cparams(("arbitrary",)),
        name=f"post_attention_l{l}",
    )(o, w_out, x, mods, norm_g, mods, mods, router_wt, router_b)


def _row_copy(src, src_row, dst, dst_row, sem):
    return pltpu.make_async_copy(src.at[pl.ds(src_row, 1), :], dst.at[pl.ds(dst_row, 1), :], sem)


def _dispatch_kernel(dest_ref, h_ref, xs_in_ref, xs_ref, sem):
    del xs_in_ref
    tm = h_ref.shape[0]

    def start(n, carry):
        for k in range(TOP_K):
            _row_copy(h_ref, n, xs_ref, dest_ref[k, n], sem).start()
        return carry

    lax.fori_loop(0, tm, start, 0)

    def wait(n, carry):
        for k in range(TOP_K):
            _row_copy(h_ref, 0, xs_ref, 0, sem).wait()
        return carry

    lax.fori_loop(0, tm, wait, 0)


def _dispatch(dest, h, n_slots):
    tm = TM_ROUTE
    n_rows = h.shape[0]
    xs0 = jnp.zeros((n_slots, D_MODEL), F32)
    return pl.pallas_call(
        _dispatch_kernel,
        grid=(n_rows // tm,),
        in_specs=[
            pl.BlockSpec((TOP_K, tm), lambda i: (0, i), memory_space=pltpu.SMEM),
            pl.BlockSpec((tm, D_MODEL), lambda i: (i, 0)),
            pl.BlockSpec(memory_space=pl.ANY),
        ],
        out_specs=pl.BlockSpec(memory_space=pl.ANY),
        out_shape=jax.ShapeDtypeStruct((n_slots, D_MODEL), F32),
        scratch_shapes=[pltpu.SemaphoreType.DMA],
        input_output_aliases={2: 0},
        compiler_params=_cparams(("arbitrary",)),
        name="moe_dispatch",
    )(dest, h, xs0)


def _expert_kernel(be_ref, nu_ref, xs_ref, wg_ref, bg_ref, wu_ref, bu_ref, wd_ref, bd_ref, ys_ref,
                   wg_s, wu_s, wd_s):
    b = pl.program_id(0)
    e = be_ref[b]
    prev = be_ref[jnp.maximum(b - 1, 0)]

    @pl.when((b == 0) | (e != prev))
    def _():
        wg_s[...] = wg_ref[0, 0].astype(BF16)
        wu_s[...] = wu_ref[0, 0].astype(BF16)
        wd_s[...] = wd_ref[0, 0].astype(BF16)

    @pl.when(b < nu_ref[0])
    def _():
        x = xs_ref[...].astype(BF16)
        g = _dot(x, wg_s[...]) + bg_ref[0, 0]
        u = _dot(x, wu_s[...]) + bu_ref[0, 0]
        g = jnp.minimum(g, SWIGLU_LIMIT)
        u = jnp.clip(u, -SWIGLU_LIMIT, SWIGLU_LIMIT)
        a = g * jax.nn.sigmoid(SWIGLU_ALPHA * g) * (u + 1.0)
        ys_ref[...] = _dot(a.astype(BF16), wd_s[...]) + bd_ref[0, 0]

    @pl.when(b >= nu_ref[0])
    def _():
        ys_ref[...] = jnp.zeros_like(ys_ref)


def _experts(block_expert, n_used, xs, l, w_gate, b_gate, w_up, b_up, w_down, b_down):
    tm = TM_EXPERT
    n_blocks = xs.shape[0] // tm
    w_spec = pl.BlockSpec((1, 1, D_MODEL, D_MODEL), lambda b, be, nu: (l, be[b], 0, 0))
    b_spec = pl.BlockSpec((1, 1, 1, D_MODEL), lambda b, be, nu: (l, be[b], 0, 0))
    grid_spec = pltpu.PrefetchScalarGridSpec(
        num_scalar_prefetch=2,
        grid=(n_blocks,),
        in_specs=[
            pl.BlockSpec((tm, D_MODEL), lambda b, be, nu: (jnp.minimum(b, nu[0] - 1), 0)),
            w_spec, b_spec, w_spec, b_spec, w_spec, b_spec,
        ],
        out_specs=pl.BlockSpec((tm, D_MODEL), lambda b, be, nu: (b, 0)),
        scratch_shapes=[pltpu.VMEM((D_MODEL, D_MODEL), BF16)] * 3,
    )
    return pl.pallas_call(
        _expert_kernel,
        grid_spec=grid_spec,
        out_shape=jax.ShapeDtypeStruct(xs.shape, F32),
        compiler_params=_cparams(("arbitrary",)),
        name=f"moe_experts_l{l}",
    )(block_expert, n_used, xs, w_gate, b_gate, w_up, b_up, w_down, b_down)


def _combine_kernel(dest_ref, gt_ref, x_ref, g2_ref, ys_ref, *rest, final):
    if final:
        gout_ref, o_ref, ybuf, sem = rest
    else:
        o_ref, ybuf, sem = rest
    tm = x_ref.shape[0]

    def start(n, carry):
        for k in range(TOP_K):
            _row_copy(ys_ref, dest_ref[k, n], ybuf.at[k], n, sem).start()
        return carry

    lax.fori_loop(0, tm, start, 0)

    def wait(n, carry):
        for k in range(TOP_K):
            _row_copy(ys_ref, 0, ybuf.at[k], 0, sem).wait()
        return carry

    lax.fori_loop(0, tm, wait, 0)

    gt = gt_ref[...]
    f = gt[:, 0:1] * ybuf[0]
    for k in range(1, TOP_K):
        f = f + gt[:, k:k + 1] * ybuf[k]
    x = x_ref[...] + g2_ref[0, 0, 0] * f
    if final:
        x = _rmsnorm(x, gout_ref[...])
    o_ref[...] = x


def _combine(dest, gates_t, x, mods, l, ys, norm_out, batch):
    tm = TM_ROUTE
    n_rows = x.shape[0]
    tiles_per_batch = SEQ // tm
    final = norm_out is not None
    row_spec = pl.BlockSpec((tm, D_MODEL), lambda i: (i, 0))
    in_specs = [
        pl.BlockSpec((TOP_K, tm), lambda i: (0, i), memory_space=pltpu.SMEM),
        pl.BlockSpec((tm, TOP_K), lambda i: (i, 0)),
        row_spec,
        _mod_spec(l, 5, tiles_per_batch, batch),
        pl.BlockSpec(memory_space=pl.ANY),
    ]
    args = [dest, gates_t, x, mods, ys]
    if final:
        in_specs.append(pl.BlockSpec((1, D_MODEL), lambda i: (0, 0)))
        args.append(norm_out)
    return pl.pallas_call(
        functools.partial(_combine_kernel, final=final),
        grid=(n_rows // tm,),
        in_specs=in_specs,
        out_specs=row_spec,
        out_shape=jax.ShapeDtypeStruct((n_rows, D_MODEL), F32),
        scratch_shapes=[pltpu.VMEM((TOP_K, tm, D_MODEL), F32), pltpu.SemaphoreType.DMA],
        compiler_params=_cparams(("arbitrary",)),
        name=f"moe_combine_l{l}",
    )(*args)


def _moe(h, idx, gates, rank, counts, x, mods, l, moe_params, norm_out, batch):
    n_rows = h.shape[0]
    tm = TM_EXPERT
    n_blocks = -(-n_rows * TOP_K // tm) + N_EXPERTS
    cnt = counts[:, 0].astype(I32)
    padded = (cnt + tm - 1) // tm * tm
    pad_end = jnp.cumsum(padded)
    pad_start = pad_end - padded
    dest = pad_start[idx] + rank
    n_used = pad_end[-1] // tm
    blk = jnp.arange(n_blocks, dtype=I32)
    be = jnp.minimum(jnp.searchsorted(pad_end, blk * tm, side="right"), N_EXPERTS - 1).astype(I32)
    be = jnp.where(blk < n_used, be, be[jnp.maximum(n_used - 1, 0)])
    xs = _dispatch(dest, h, n_blocks * tm)
    ys = _experts(be, n_used.reshape(1).astype(I32), xs, l, *moe_params)
    return _combine(dest, gates.T, x, mods, l, ys, norm_out, batch)


def _rope_tables(rot_dim):
    t = jnp.arange(SEQ)
    row = (t // GRID_W).astype(F32)
    col = (t % GRID_W).astype(F32)
    n_freq = rot_dim // 4
    inv = ROPE_THETA ** (-jnp.arange(n_freq, dtype=F32) / n_freq)
    ang = jnp.concatenate([row[:, None] * inv, col[:, None] * inv], axis=-1)
    cos = jnp.repeat(jnp.cos(ang), 2, axis=-1)
    sin = jnp.repeat(jnp.sin(ang), 2, axis=-1) * jnp.tile(jnp.array([-1.0, 1.0], F32), rot_dim // 2)
    reps = LANES // rot_dim
    cos = jnp.concatenate([jnp.tile(cos, (1, reps)), jnp.ones((TM_PROJ, LANES), F32)], axis=0)
    sin = jnp.concatenate([jnp.tile(sin, (1, reps)), jnp.zeros((TM_PROJ, LANES), F32)], axis=0)
    return cos, sin


def _gqa_weights(w_in):
    qd = A_HEADS * A_HEAD_DIM
    kd = A_KV_HEADS * A_HEAD_DIM
    wq = w_in[:, :qd]
    wk = w_in[:, qd:qd + kd].reshape(D_MODEL, A_KV_HEADS, 1, A_HEAD_DIM)
    wv = w_in[:, qd + kd:].reshape(D_MODEL, A_KV_HEADS, 1, A_HEAD_DIM)
    dup = lambda w: jnp.broadcast_to(w, (D_MODEL, A_KV_HEADS, 2, A_HEAD_DIM)).reshape(D_MODEL, 2 * kd)
    return jnp.concatenate([wq, dup(wk), dup(wv)], axis=1).astype(BF16)


def _mla_weights(w_in, w_uq, w_ukv):
    w_in_p = jnp.pad(w_in, ((0, 0), (0, 4 * LANES - w_in.shape[1]))).astype(BF16)
    uq = w_uq.reshape(C_Q_LORA, C_HEADS, C_NOPE + C_ROPE)
    uq = jnp.concatenate([uq[:, :, :C_NOPE].reshape(C_Q_LORA, -1), uq[:, :, C_NOPE:].reshape(C_Q_LORA, -1)], axis=1)
    ukv = w_ukv.reshape(C_KV_LORA, C_HEADS, C_NOPE + C_V)
    ukv = jnp.concatenate([ukv[:, :, :C_NOPE].reshape(C_KV_LORA, -1), ukv[:, :, C_NOPE:].reshape(C_KV_LORA, -1)], axis=1)
    return w_in_p, uq.astype(BF16), ukv.astype(BF16)


def _na_bias(rpb):
    tq = NA_TILE_ROWS * GRID_W
    ks = NA_SLAB_ROWS * GRID_W
    qa, qc = np.divmod(np.arange(tq), GRID_W)
    ki, kc = np.divmod(np.arange(ks), GRID_W)
    dr_all, dc_all, ok_all = [], [], []
    for r_base, s_row in ((0, 0), (NA_TILE_ROWS, 0), (GRID_H - NA_TILE_ROWS, GRID_H - NA_SLAB_ROWS)):
        r = (r_base + qa)[:, None]
        rk = (s_row + ki)[None, :]
        r0 = np.clip(r - NA_ROWS // 2, 0, GRID_H - NA_ROWS)
        wstart = np.clip(qc - NA_COLS // 2, 0, GRID_W - NA_COLS)[:, None]
        ok = (rk >= r0) & (rk < r0 + NA_ROWS) & (kc[None, :] >= wstart) & (kc[None, :] < wstart + NA_COLS)
        dr_all.append(np.clip(rk - r + NA_ROWS - 1, 0, 2 * NA_ROWS - 2))
        dc_all.append(np.clip(kc[None, :] - qc[:, None] + NA_COLS - 1, 0, 2 * NA_COLS - 2))
        ok_all.append(ok)
    dr = np.stack(dr_all)
    dc = np.stack(dc_all)
    ok = np.stack(ok_all)
    return jnp.where(jnp.asarray(ok)[None], rpb.astype(F32)[:, dr, dc], NEG_INF)


def kernel(x, c, ctx, c_ctx, ada_w, ada_b, norm_mix, norm_ffn, norm_out, a_w_in, a_w_out, a_sink, b_w_in, b_w_out, b_rpb, c_w_in, c_q_norm, c_kv_norm, c_w_uq, c_w_ukv, c_w_out, router_w, router_b, moe_w_gate, moe_b_gate, moe_w_up, moe_b_up, moe_w_down, moe_b_down):
    batch = x.shape[0]
    assert x.shape[1:] == (SEQ, D_MODEL) and ctx.shape[1:] == (CTX_LEN, D_MODEL)
    assert batch + 1 <= MOD_ROWS
    n_lat = batch * SEQ
    n_ctx = batch * CTX_LEN

    cc = jnp.concatenate([c, c_ctx[None, :], jnp.zeros((MOD_ROWS - batch - 1, D_MODEL), F32)], axis=0)
    mods = _modulation(cc, ada_w, ada_b).reshape(DEPTH, MOD_ROWS, 6, 1, D_MODEL)

    norm_mix3 = norm_mix.reshape(DEPTH, 1, D_MODEL)
    norm_ffn3 = norm_ffn.reshape(DEPTH, 1, D_MODEL)
    router_wt = jnp.swapaxes(router_w, 1, 2)
    router_b3 = router_b.reshape(DEPTH, N_EXPERTS, 1)
    moe_biases = [b.reshape(DEPTH, N_EXPERTS, 1, D_MODEL) for b in (moe_b_gate, moe_b_up, moe_b_down)]
    moe_params = (moe_w_gate, moe_biases[0], moe_w_up, moe_biases[1], moe_w_down, moe_biases[2])
    rope_a = _rope_tables(A_HEAD_DIM)
    rope_c = _rope_tables(C_ROPE)

    xs = jnp.concatenate([x.reshape(n_lat, D_MODEL), ctx.reshape(n_ctx, D_MODEL)], axis=0)
    for l in range(DEPTH):
        last = l == DEPTH - 1
        kind, slot = l % N_MIXERS, l // N_MIXERS
        if kind == 0:
            y = _project(xs, norm_mix3, mods, l, _gqa_weights(a_w_in[slot]), rope_a, N_PAIRS + A_KV_HEADS, batch)
            o = _gqa_attention(y, a_sink[slot], batch)
            oc = None if last else _gqa_ctx_attention(y, a_sink[slot], batch)
            w_out = a_w_out[slot]
        elif kind == 1:
            y = _project(xs, norm_mix3, mods, l, b_w_in[slot].astype(BF16), None, 0, batch)
            o = _na_attention(y, _na_bias(b_rpb[slot]), batch)
            oc = None if last else _na_ctx_attention(y, batch)
            w_out = b_w_out[slot]
        else:
            w_in_p, uq, ukv = _mla_weights(c_w_in[slot], c_w_uq[slot], c_w_ukv[slot])
            y = _mla_project(xs, norm_mix3, mods, l, w_in_p, c_q_norm[slot][None, :], c_kv_norm[slot][None, :],
                             uq, ukv, rope_c, batch)
            o = _mla_attention(y, batch)
            oc = None if last else _mla_ctx_attention(y, batch)
            w_out = c_w_out[slot]
        if not last:
            o = jnp.concatenate([o, oc], axis=1)
        n_rows = o.shape[1]
        x_mid, h, idx, gates, rank, counts = _post_attention(
            o, w_out.astype(BF16), xs, mods, l, norm_ffn3, router_wt, router_b3, n_rows, batch)
        xs = _moe(h, idx, gates, rank, counts, x_mid, mods, l, moe_params,
                  norm_out[None, :] if last else None, batch)
    return xs.reshape(batch, SEQ, D_MODEL)
```

```python
import functools

import numpy as np
import jax
import jax.numpy as jnp
from jax import lax
from jax.experimental import pallas as pl
from jax.experimental.pallas import tpu as pltpu

F32 = jnp.float32
BF16 = jnp.bfloat16
I32 = jnp.int32

D_MODEL = 1024
DEPTH = 4
SEQ = 2048
GRID_W = 64
GRID_H = SEQ // GRID_W
CTX_LEN = 256
N_MIXERS = 3
NORM_EPS = 1e-6
ROPE_THETA = 10000.0
NEG_INF = -1e30

A_HEADS, A_KV_HEADS, A_HEAD_DIM, A_WINDOW = 16, 4, 64, 128
B_HEADS, B_HEAD_DIM, NA_ROWS, NA_COLS = 16, 64, 8, 16
C_HEADS, C_NOPE, C_ROPE, C_V, C_Q_LORA, C_KV_LORA = 16, 64, 32, 64, 256, 128
N_EXPERTS, TOP_K = 32, 4
SWIGLU_LIMIT, SWIGLU_ALPHA = 7.0, 1.702

LANES = 128
N_PAIRS = 8
MOD_ROWS = 16

TM_PROJ = 512
TM_POST = 256
TM_ROUTE = 256
TM_EXPERT = 512
TQ_GQA = 256
TQ_MLA = 256
NA_TILE_ROWS = 4
NA_SLAB_ROWS = 12
VMEM_MB = 56


def _cparams(sem, vmem_mb=VMEM_MB):
    return pltpu.CompilerParams(dimension_semantics=sem, vmem_limit_bytes=vmem_mb * 1024 * 1024)


def _dot(a, b):
    return jnp.dot(a, b, preferred_element_type=F32)


def _dot_nt(a, b):
    return lax.dot_general(a, b, (((1,), (1,)), ((), ())), preferred_element_type=F32)


def _split(x):
    hi = x.astype(BF16)
    lo = (x - hi.astype(F32)).astype(BF16)
    return hi, lo


def _norm_mod(x, g, shift, scale):
    ms = jnp.mean(x * x, axis=-1, keepdims=True)
    y = x * lax.rsqrt(ms + NORM_EPS) * g
    return y * (1.0 + scale) + shift


def _rmsnorm(x, g):
    ms = jnp.mean(x * x, axis=-1, keepdims=True)
    return x * lax.rsqrt(ms + NORM_EPS) * g


def _rope_block(xb, cos, sin_signed, even):
    nxt = pltpu.roll(xb, LANES - 1, 1)
    prv = pltpu.roll(xb, 1, 1)
    return xb * cos + jnp.where(even, nxt, prv) * sin_signed


def _softmax_pv(s_parts, v_parts, sink):
    m = s_parts[0].max(axis=-1, keepdims=True)
    for s in s_parts[1:]:
        m = jnp.maximum(m, s.max(axis=-1, keepdims=True))
    if sink is not None:
        m = jnp.maximum(m, sink)
    den = None
    acc = None
    for s, v in zip(s_parts, v_parts):
        e = jnp.exp(s - m)
        d = e.sum(axis=-1, keepdims=True)
        a = _dot(e.astype(BF16), v)
        den = d if den is None else den + d
        acc = a if acc is None else acc + a
    if sink is not None:
        den = den + jnp.exp(sink - m)
    return acc / den


def _two_heads(qs, parts, scale, sinks, lo):
    outs = []
    for half in range(2):
        s_list = []
        for k, _, post in parts:
            s = _dot_nt(qs[half], k) * scale
            if post is not None:
                s = post(s, half)
            s_list.append(s)
        sink = None if sinks is None else sinks[half]
        outs.append(_softmax_pv(s_list, [p[1] for p in parts], sink))
    return jnp.where(lo, outs[0], outs[1]).astype(BF16)


def _lane_masks(rows):
    lane = lax.broadcasted_iota(I32, (rows, LANES), 1)
    return lane, lane < (LANES // 2)


def _mask_halves(q, lo):
    z = jnp.zeros_like(q)
    return [jnp.where(lo, q, z), jnp.where(lo, z, q)]


def _mod_kernel(c_ref, w_ref, b_ref, o_ref):
    c = c_ref[...]
    s = c * jax.nn.sigmoid(c)
    s_hi, s_lo = _split(s)
    w_hi, w_lo = _split(w_ref[0])
    o_ref[0] = _dot(s_hi, w_hi) + _dot(s_lo, w_hi) + _dot(s_hi, w_lo) + b_ref[0]


def _modulation(cc, ada_w, ada_b):
    tn = 1536
    n6 = ada_w.shape[-1]
    return pl.pallas_call(
        _mod_kernel,
        grid=(DEPTH, n6 // tn),
        in_specs=[
            pl.BlockSpec((MOD_ROWS, D_MODEL), lambda l, j: (0, 0)),
            pl.BlockSpec((1, D_MODEL, tn), lambda l, j: (l, 0, j)),
            pl.BlockSpec((1, 1, tn), lambda l, j: (l, 0, j)),
        ],
        out_specs=pl.BlockSpec((1, MOD_ROWS, tn), lambda l, j: (l, 0, j)),
        out_shape=jax.ShapeDtypeStruct((DEPTH, MOD_ROWS, n6), F32),
        compiler_params=_cparams(("arbitrary", "arbitrary")),
        name="modulation",
    )(cc, ada_w, ada_b.reshape(DEPTH, 1, n6))


def _mod_spec(l, which, n_lat_tiles_per_batch, batch):
    def index(i):
        return (l, jnp.minimum(i // n_lat_tiles_per_batch, batch), which, 0, 0)
    return pl.BlockSpec((1, 1, 1, 1, D_MODEL), index)


def _proj_kernel(x_ref, g_ref, sh_ref, sc_ref, w_ref, *rest, n_rope):
    if n_rope:
        cos_ref, sin_ref, o_ref = rest
    else:
        (o_ref,) = rest
    h = _norm_mod(x_ref[...], g_ref[0], sh_ref[0, 0, 0], sc_ref[0, 0, 0])
    y = _dot(h.astype(BF16), w_ref[...])
    if n_rope:
        cos = cos_ref[...]
        sin = sin_ref[...]
        lane, _ = _lane_masks(x_ref.shape[0])
        even = (lane & 1) == 0
    for j in range(o_ref.shape[0]):
        yb = y[:, j * LANES:(j + 1) * LANES]
        if j < n_rope:
            yb = _rope_block(yb, cos, sin, even)
        o_ref[j] = yb.astype(BF16)


def _project(x, norm_g, mods, l, w, rope, n_rope, batch):
    n_rows = x.shape[0]
    tm = TM_PROJ
    n_tiles = n_rows // tm
    tiles_per_batch = SEQ // tm
    n_lat_tiles = batch * tiles_per_batch
    ncb = w.shape[1] // LANES
    in_specs = [
        pl.BlockSpec((tm, D_MODEL), lambda i: (i, 0)),
        pl.BlockSpec((1, 1, D_MODEL), lambda i: (l, 0, 0)),
        _mod_spec(l, 0, tiles_per_batch, batch),
        _mod_spec(l, 1, tiles_per_batch, batch),
        pl.BlockSpec(w.shape, lambda i: (0, 0)),
    ]
    args = [x, norm_g, mods, mods, w]
    if n_rope:
        def rope_index(i):
            return (jnp.where(i < n_lat_tiles, i % tiles_per_batch, tiles_per_batch), 0)
        in_specs += [pl.BlockSpec((tm, LANES), rope_index)] * 2
        args += list(rope)
    return pl.pallas_call(
        functools.partial(_proj_kernel, n_rope=n_rope),
        grid=(n_tiles,),
        in_specs=in_specs,
        out_specs=pl.BlockSpec((ncb, tm, LANES), lambda i: (0, i, 0)),
        out_shape=jax.ShapeDtypeStruct((ncb, n_rows, LANES), BF16),
        compiler_params=_cparams(("arbitrary",)),
        name=f"project_l{l}",
    )(*args)


def _mla_proj_kernel(x_ref, g_ref, sh_ref, sc_ref, w_ref, qn_ref, kvn_ref, wq_ref, wkv_ref,
                     cos_ref, sin_ref, o_ref):
    h = _norm_mod(x_ref[...], g_ref[0], sh_ref[0, 0, 0], sc_ref[0, 0, 0])
    y = _dot(h.astype(BF16), w_ref[...])
    cq = _rmsnorm(y[:, :C_Q_LORA], qn_ref[...])
    qq = _dot(cq.astype(BF16), wq_ref[...])
    ckv = _rmsnorm(y[:, C_Q_LORA:C_Q_LORA + C_KV_LORA], kvn_ref[...])
    kv = _dot(ckv.astype(BF16), wkv_ref[...])
    kr = y[:, C_Q_LORA + C_KV_LORA:]
    kr = kr + pltpu.roll(kr, 32, 1) + pltpu.roll(kr, 64, 1) + pltpu.roll(kr, 96, 1)
    cos = cos_ref[...]
    sin = sin_ref[...]
    lane, _ = _lane_masks(x_ref.shape[0])
    even = (lane & 1) == 0
    for j in range(N_PAIRS):
        o_ref[j] = qq[:, j * LANES:(j + 1) * LANES].astype(BF16)
    for j in range(2 * N_PAIRS):
        o_ref[N_PAIRS + j] = kv[:, j * LANES:(j + 1) * LANES].astype(BF16)
    for j in range(4):
        qb = qq[:, (N_PAIRS + j) * LANES:(N_PAIRS + j + 1) * LANES]
        o_ref[3 * N_PAIRS + j] = _rope_block(qb, cos, sin, even).astype(BF16)
    o_ref[3 * N_PAIRS + 4] = _rope_block(kr, cos, sin, even).astype(BF16)


def _mla_project(x, norm_g, mods, l, w_in, q_norm, kv_norm, w_uq, w_ukv, rope, batch):
    n_rows = x.shape[0]
    tm = TM_PROJ
    tiles_per_batch = SEQ // tm
    n_lat_tiles = batch * tiles_per_batch
    ncb = 3 * N_PAIRS + 5

    def rope_index(i):
        return (jnp.where(i < n_lat_tiles, i % tiles_per_batch, tiles_per_batch), 0)

    def full(a):
        return pl.BlockSpec(a.shape, lambda i: (0,) * a.ndim)

    return pl.pallas_call(
        _mla_proj_kernel,
        grid=(n_rows // tm,),
        in_specs=[
            pl.BlockSpec((tm, D_MODEL), lambda i: (i, 0)),
            pl.BlockSpec((1, 1, D_MODEL), lambda i: (l, 0, 0)),
            _mod_spec(l, 0, tiles_per_batch, batch),
            _mod_spec(l, 1, tiles_per_batch, batch),
            full(w_in), full(q_norm), full(kv_norm), full(w_uq), full(w_ukv),
            pl.BlockSpec((tm, LANES), rope_index),
            pl.BlockSpec((tm, LANES), rope_index),
        ],
        out_specs=pl.BlockSpec((ncb, tm, LANES), lambda i: (0, i, 0)),
        out_shape=jax.ShapeDtypeStruct((ncb, n_rows, LANES), BF16),
        compiler_params=_cparams(("arbitrary",)),
        name=f"mla_project_l{l}",
    )(x, norm_g, mods, mods, w_in, q_norm, kv_norm, w_uq, w_ukv, *rope)


def _gqa_kernel(sink_ref, q_ref, k_ref, v_ref, kc_ref, vc_ref, o_ref, *, tq):
    j = pl.program_id(1)
    ks = tq + 2 * A_WINDOW
    s0 = pl.multiple_of(jnp.clip(j * tq - A_WINDOW, 0, SEQ - ks), LANES)
    qpos = j * tq + lax.broadcasted_iota(I32, (tq, ks), 0)
    kpos = s0 + lax.broadcasted_iota(I32, (tq, ks), 1)
    band = jnp.abs(kpos - qpos) <= A_WINDOW
    _, lo = _lane_masks(tq)
    scale = A_HEAD_DIM ** -0.5

    def masked(s, half):
        return jnp.where(band, s, NEG_INF)

    def pair(c, carry):
        kh = c // 2
        k = k_ref[kh, pl.ds(s0, ks), :]
        v = v_ref[kh, pl.ds(s0, ks), :]
        parts = [(k, v, masked), (kc_ref[kh], vc_ref[kh], None)]
        sinks = [sink_ref[2 * c], sink_ref[2 * c + 1]]
        o_ref[c] = _two_heads(_mask_halves(q_ref[c], lo), parts, scale, sinks, lo)
        return carry

    lax.fori_loop(0, N_PAIRS, pair, 0)


def _gqa_attention(y, sink, batch):
    tq = TQ_GQA
    n_lat = batch * SEQ
    ctx0 = n_lat // CTX_LEN
    kvb = A_KV_HEADS
    return pl.pallas_call(
        functools.partial(_gqa_kernel, tq=tq),
        grid=(batch, SEQ // tq),
        in_specs=[
            pl.BlockSpec(memory_space=pltpu.SMEM),
            pl.BlockSpec((N_PAIRS, tq, LANES), lambda b, j: (0, b * (SEQ // tq) + j, 0)),
            pl.BlockSpec((kvb, SEQ, LANES), lambda b, j: (2, b, 0)),
            pl.BlockSpec((kvb, SEQ, LANES), lambda b, j: (3, b, 0)),
            pl.BlockSpec((kvb, CTX_LEN, LANES), lambda b, j: (2, ctx0 + b, 0)),
            pl.BlockSpec((kvb, CTX_LEN, LANES), lambda b, j: (3, ctx0 + b, 0)),
        ],
        out_specs=pl.BlockSpec((N_PAIRS, tq, LANES), lambda b, j: (0, b * (SEQ // tq) + j, 0)),
        out_shape=jax.ShapeDtypeStruct((N_PAIRS, n_lat, LANES), BF16),
        compiler_params=_cparams(("arbitrary", "arbitrary")),
        name="gqa_attention",
    )(sink, y, y, y, y, y)


def _gqa_ctx_kernel(sink_ref, q_ref, k_ref, v_ref, o_ref):
    _, lo = _lane_masks(CTX_LEN)

    def pair(c, carry):
        kh = c // 2
        sinks = [sink_ref[2 * c], sink_ref[2 * c + 1]]
        o_ref[c] = _two_heads(_mask_halves(q_ref[c], lo), [(k_ref[kh], v_ref[kh], None)],
                              A_HEAD_DIM ** -0.5, sinks, lo)
        return carry

    lax.fori_loop(0, N_PAIRS, pair, 0)


def _gqa_ctx_attention(y, sink, batch):
    ctx0 = batch * SEQ // CTX_LEN
    kvb = A_KV_HEADS
    return pl.pallas_call(
        _gqa_ctx_kernel,
        grid=(batch,),
        in_specs=[
            pl.BlockSpec(memory_space=pltpu.SMEM),
            pl.BlockSpec((N_PAIRS, CTX_LEN, LANES), lambda b: (0, ctx0 + b, 0)),
            pl.BlockSpec((kvb, CTX_LEN, LANES), lambda b: (2, ctx0 + b, 0)),
            pl.BlockSpec((kvb, CTX_LEN, LANES), lambda b: (3, ctx0 + b, 0)),
        ],
        out_specs=pl.BlockSpec((N_PAIRS, CTX_LEN, LANES), lambda b: (0, b, 0)),
        out_shape=jax.ShapeDtypeStruct((N_PAIRS, batch * CTX_LEN, LANES), BF16),
        compiler_params=_cparams(("arbitrary",)),
        name="gqa_ctx_attention",
    )(sink, y, y, y)


def _na_kernel(q_ref, k_ref, v_ref, kc_ref, vc_ref, bias_ref, o_ref, *, tq, ks):
    t = pl.program_id(1)
    s0row = jnp.clip(NA_TILE_ROWS * t - NA_ROWS // 2, 0, GRID_H - NA_SLAB_ROWS)
    s0 = pl.multiple_of(s0row * GRID_W, GRID_W)
    _, lo = _lane_masks(tq)

    def pair(c, carry):
        k = k_ref[c, pl.ds(s0, ks), :]
        v = v_ref[c, pl.ds(s0, ks), :]

        def biased(s, half):
            return s + bias_ref[2 * c + half, 0]

        parts = [(k, v, biased), (kc_ref[c], vc_ref[c], None)]
        o_ref[c] = _two_heads(_mask_halves(q_ref[c], lo), parts, B_HEAD_DIM ** -0.5, None, lo)
        return carry

    lax.fori_loop(0, N_PAIRS, pair, 0)


def _na_attention(y, bias, batch):
    tq = NA_TILE_ROWS * GRID_W
    ks = NA_SLAB_ROWS * GRID_W
    n_t = SEQ // tq
    n_lat = batch * SEQ
    ctx0 = n_lat // CTX_LEN

    def bias_index(b, t):
        return (0, jnp.where(t == 0, 0, jnp.where(t == n_t - 1, 2, 1)), 0, 0)

    return pl.pallas_call(
        functools.partial(_na_kernel, tq=tq, ks=ks),
        grid=(batch, n_t),
        in_specs=[
            pl.BlockSpec((N_PAIRS, tq, LANES), lambda b, t: (0, b * n_t + t, 0)),
            pl.BlockSpec((N_PAIRS, SEQ, LANES), lambda b, t: (1, b, 0)),
            pl.BlockSpec((N_PAIRS, SEQ, LANES), lambda b, t: (2, b, 0)),
            pl.BlockSpec((N_PAIRS, CTX_LEN, LANES), lambda b, t: (1, ctx0 + b, 0)),
            pl.BlockSpec((N_PAIRS, CTX_LEN, LANES), lambda b, t: (2, ctx0 + b, 0)),
            pl.BlockSpec((B_HEADS, 1, tq, ks), bias_index),
        ],
        out_specs=pl.BlockSpec((N_PAIRS, tq, LANES), lambda b, t: (0, b * n_t + t, 0)),
        out_shape=jax.ShapeDtypeStruct((N_PAIRS, n_lat, LANES), BF16),
        compiler_params=_cparams(("arbitrary", "arbitrary"), 60),
        name="na_attention",
    )(y, y, y, y, y, bias)


def _na_ctx_kernel(q_ref, k_ref, v_ref, o_ref):
    _, lo = _lane_masks(CTX_LEN)

    def pair(c, carry):
        o_ref[c] = _two_heads(_mask_halves(q_ref[c], lo), [(k_ref[c], v_ref[c], None)],
                              B_HEAD_DIM ** -0.5, None, lo)
        return carry

    lax.fori_loop(0, N_PAIRS, pair, 0)


def _na_ctx_attention(y, batch):
    ctx0 = batch * SEQ // CTX_LEN
    return pl.pallas_call(
        _na_ctx_kernel,
        grid=(batch,),
        in_specs=[
            pl.BlockSpec((N_PAIRS, CTX_LEN, LANES), lambda b: (0, ctx0 + b, 0)),
            pl.BlockSpec((N_PAIRS, CTX_LEN, LANES), lambda b: (1, ctx0 + b, 0)),
            pl.BlockSpec((N_PAIRS, CTX_LEN, LANES), lambda b: (2, ctx0 + b, 0)),
        ],
        out_specs=pl.BlockSpec((N_PAIRS, CTX_LEN, LANES), lambda b: (0, b, 0)),
        out_shape=jax.ShapeDtypeStruct((N_PAIRS, batch * CTX_LEN, LANES), BF16),
        compiler_params=_cparams(("arbitrary",)),
        name="na_ctx_attention",
    )(y, y, y)


def _mla_queries(qn, qr, c, lane, lo):
    zero = jnp.zeros_like(qn)
    quarter = lane // C_ROPE
    qs = []
    for half in range(2):
        u = 2 * (c % 2) + half
        qn_m = jnp.where(lo, qn, zero) if half == 0 else jnp.where(lo, zero, qn)
        qr_m = jnp.where(quarter == u, qr, zero)
        qs.append(jnp.concatenate([qn_m, qr_m], axis=1))
    return qs


def _mla_kernel(qn_ref, qr_ref, kn_ref, v_ref, kr_ref, knc_ref, vc_ref, krc_ref, o_ref, *, tq):
    lane, lo = _lane_masks(tq)
    scale = (C_NOPE + C_ROPE) ** -0.5

    def pair(c, carry):
        kcat = jnp.concatenate([kn_ref[c], kr_ref[0]], axis=1)
        kccat = jnp.concatenate([knc_ref[c], krc_ref[0]], axis=1)
        parts = [(kccat, vc_ref[c], None), (kcat, v_ref[c], None)]
        qs = _mla_queries(qn_ref[c], qr_ref[c // 2], c, lane, lo)
        o_ref[c] = _two_heads(qs, parts, scale, None, lo)
        return carry

    lax.fori_loop(0, N_PAIRS, pair, 0)


def _mla_attention(y, batch):
    tq = TQ_MLA
    n_t = SEQ // tq
    n_lat = batch * SEQ
    ctx0 = n_lat // CTX_LEN
    kr_blk = 3 * N_PAIRS + 4
    return pl.pallas_call(
        functools.partial(_mla_kernel, tq=tq),
        grid=(batch, n_t),
        in_specs=[
            pl.BlockSpec((N_PAIRS, tq, LANES), lambda b, t: (0, b * n_t + t, 0)),
            pl.BlockSpec((4, tq, LANES), lambda b, t: (6, b * n_t + t, 0)),
            pl.BlockSpec((N_PAIRS, SEQ, LANES), lambda b, t: (1, b, 0)),
            pl.BlockSpec((N_PAIRS, SEQ, LANES), lambda b, t: (2, b, 0)),
            pl.BlockSpec((1, SEQ, LANES), lambda b, t: (kr_blk, b, 0)),
            pl.BlockSpec((N_PAIRS, CTX_LEN, LANES), lambda b, t: (1, ctx0 + b, 0)),
            pl.BlockSpec((N_PAIRS, CTX_LEN, LANES), lambda b, t: (2, ctx0 + b, 0)),
            pl.BlockSpec((1, CTX_LEN, LANES), lambda b, t: (kr_blk, ctx0 + b, 0)),
        ],
        out_specs=pl.BlockSpec((N_PAIRS, tq, LANES), lambda b, t: (0, b * n_t + t, 0)),
        out_shape=jax.ShapeDtypeStruct((N_PAIRS, n_lat, LANES), BF16),
        compiler_params=_cparams(("arbitrary", "arbitrary")),
        name="mla_attention",
    )(y, y, y, y, y, y, y, y)


def _mla_ctx_kernel(qn_ref, qr_ref, kn_ref, v_ref, kr_ref, o_ref):
    lane, lo = _lane_masks(CTX_LEN)

    def pair(c, carry):
        kcat = jnp.concatenate([kn_ref[c], kr_ref[0]], axis=1)
        qs = _mla_queries(qn_ref[c], qr_ref[c // 2], c, lane, lo)
        o_ref[c] = _two_heads(qs, [(kcat, v_ref[c], None)], (C_NOPE + C_ROPE) ** -0.5, None, lo)
        return carry

    lax.fori_loop(0, N_PAIRS, pair, 0)


def _mla_ctx_attention(y, batch):
    ctx0 = batch * SEQ // CTX_LEN
    kr_blk = 3 * N_PAIRS + 4
    return pl.pallas_call(
        _mla_ctx_kernel,
        grid=(batch,),
        in_specs=[
            pl.BlockSpec((N_PAIRS, CTX_LEN, LANES), lambda b: (0, ctx0 + b, 0)),
            pl.BlockSpec((4, CTX_LEN, LANES), lambda b: (6, ctx0 + b, 0)),
            pl.BlockSpec((N_PAIRS, CTX_LEN, LANES), lambda b: (1, ctx0 + b, 0)),
            pl.BlockSpec((N_PAIRS, CTX_LEN, LANES), lambda b: (2, ctx0 + b, 0)),
            pl.BlockSpec((1, CTX_LEN, LANES), lambda b: (kr_blk, ctx0 + b, 0)),
        ],
        out_specs=pl.BlockSpec((N_PAIRS, CTX_LEN, LANES), lambda b: (0, b, 0)),
        out_shape=jax.ShapeDtypeStruct((N_PAIRS, batch * CTX_LEN, LANES), BF16),
        compiler_params=_cparams(("arbitrary",)),
        name="mla_ctx_attention",
    )(y, y, y, y, y)


def _post_kernel(o_ref, w_ref, x_ref, g1_ref, gn_ref, sh_ref, sc_ref, rw_ref, rb_ref,
                 xo_ref, h_ref, idx_ref, gate_ref, rank_ref, cnt_ref, run_ref):
    tm = x_ref.shape[0]

    @pl.when(pl.program_id(0) == 0)
    def _():
        run_ref[...] = jnp.zeros_like(run_ref)

    o = jnp.concatenate([o_ref[c] for c in range(N_PAIRS)], axis=1)
    x = x_ref[...] + g1_ref[0, 0, 0] * _dot(o, w_ref[...])
    xo_ref[...] = x
    h = _norm_mod(x, gn_ref[0], sh_ref[0, 0, 0], sc_ref[0, 0, 0])
    h_ref[...] = h

    h_hi, h_lo = _split(h)
    rw_hi, rw_lo = _split(rw_ref[0])
    logits = _dot_nt(rw_hi, h_hi) + _dot_nt(rw_lo, h_hi) + _dot_nt(rw_hi, h_lo) + rb_ref[0]

    ie = lax.broadcasted_iota(I32, (N_EXPERTS, tm), 0).astype(F32)
    work = logits
    vals, sels = [], []
    for k in range(TOP_K):
        m = work.max(axis=0, keepdims=True)
        idx = jnp.min(jnp.where(work == m, ie, float(N_EXPERTS)), axis=0, keepdims=True)
        sel = ie == idx
        idx_ref[k:k + 1, :] = idx.astype(I32)
        vals.append(m)
        sels.append(sel)
        work = jnp.where(sel, -3.0e38, work)
    es = [jnp.exp(v - vals[0]) for v in vals]
    den = es[0] + es[1] + es[2] + es[3]
    for k in range(TOP_K):
        gate_ref[k:k + 1, :] = es[k] / den

    onehot = sels[0].astype(F32) + sels[1].astype(F32) + sels[2].astype(F32) + sels[3].astype(F32)
    before = lax.broadcasted_iota(I32, (tm, tm), 0) < lax.broadcasted_iota(I32, (tm, tm), 1)
    prefix = _dot(onehot.astype(BF16), before.astype(F32).astype(BF16)) + run_ref[...]
    for k in range(TOP_K):
        rank = jnp.sum(jnp.where(sels[k], prefix, 0.0), axis=0, keepdims=True)
        rank_ref[k:k + 1, :] = rank.astype(I32)
    run_ref[...] = run_ref[...] + onehot.sum(axis=1, keepdims=True)
    cnt_ref[...] = jnp.broadcast_to(run_ref[...], cnt_ref.shape)


def _post_attention(o, w_out, x, mods, l, norm_g, router_wt, router_b, n_rows, batch):
    tm = TM_POST
    n_tiles = n_rows // tm
    tiles_per_batch = SEQ // tm
    row_spec = pl.BlockSpec((tm, D_MODEL), lambda i: (i, 0))
    tok_spec = pl.BlockSpec((TOP_K, tm), lambda i: (0, i))
    return pl.pallas_call(
        _post_kernel,
        grid=(n_tiles,),
        in_specs=[
            pl.BlockSpec((N_PAIRS, tm, LANES), lambda i: (0, i, 0)),
            pl.BlockSpec(w_out.shape, lambda i: (0, 0)),
            row_spec,
            _mod_spec(l, 2, tiles_per_batch, batch),
            pl.BlockSpec((1, 1, D_MODEL), lambda i: (l, 0, 0)),
            _mod_spec(l, 3, tiles_per_batch, batch),
            _mod_spec(l, 4, tiles_per_batch, batch),
            pl.BlockSpec((1, N_EXPERTS, D_MODEL), lambda i: (l, 0, 0)),
            pl.BlockSpec((1, N_EXPERTS, 1), lambda i: (l, 0, 0)),
        ],
        out_specs=[row_spec, row_spec, tok_spec, tok_spec, tok_spec,
                   pl.BlockSpec((N_EXPERTS, LANES), lambda i: (0, 0))],
        out_shape=[
            jax.ShapeDtypeStruct((n_rows, D_MODEL), F32),
            jax.ShapeDtypeStruct((n_rows, D_MODEL), F32),
            jax.ShapeDtypeStruct((TOP_K, n_rows), I32),
            jax.ShapeDtypeStruct((TOP_K, n_rows), F32),
            jax.ShapeDtypeStruct((TOP_K, n_rows), I32),
            jax.ShapeDtypeStruct((N_EXPERTS, LANES), F32),
        ],
        scratch_shapes=[pltpu.VMEM((N_EXPERTS, 1), F32)],
        compiler_params=_cparams(("arbitrary",)),
        name=f"post_attention_l{l}",
    )(o, w_out, x, mods, norm_g, mods, mods, router_wt, router_b)


def _row_copy(src, src_row, dst, dst_row, sem):
    return pltpu.make_async_copy(src.at[pl.ds(src_row, 1), :], dst.at[pl.ds(dst_row, 1), :], sem)


def _dispatch_kernel(dest_ref, h_ref, xs_in_ref, xs_ref, sem):
    del xs_in_ref
    tm = h_ref.shape[0]

    def start(n, carry):
        for k in range(TOP_K):
            _row_copy(h_ref, n, xs_ref, dest_ref[k, n], sem).start()
        return carry

    lax.fori_loop(0, tm, start, 0)

    def wait(n, carry):
        for k in range(TOP_K):
            _row_copy(h_ref, 0, xs_ref, 0, sem).wait()
        return carry

    lax.fori_loop(0, tm, wait, 0)


def _dispatch(dest, h, n_slots):
    tm = TM_ROUTE
    n_rows = h.shape[0]
    xs0 = jnp.zeros((n_slots, D_MODEL), F32)
    return pl.pallas_call(
        _dispatch_kernel,
        grid=(n_rows // tm,),
        in_specs=[
            pl.BlockSpec((TOP_K, tm), lambda i: (0, i), memory_space=pltpu.SMEM),
            pl.BlockSpec((tm, D_MODEL), lambda i: (i, 0)),
            pl.BlockSpec(memory_space=pl.ANY),
        ],
        out_specs=pl.BlockSpec(memory_space=pl.ANY),
        out_shape=jax.ShapeDtypeStruct((n_slots, D_MODEL), F32),
        scratch_shapes=[pltpu.SemaphoreType.DMA],
        input_output_aliases={2: 0},
        compiler_params=_cparams(("arbitrary",)),
        name="moe_dispatch",
    )(dest, h, xs0)


def _expert_kernel(be_ref, nu_ref, xs_ref, wg_ref, bg_ref, wu_ref, bu_ref, wd_ref, bd_ref, ys_ref,
                   wg_s, wu_s, wd_s):
    b = pl.program_id(0)
    e = be_ref[b]
    prev = be_ref[jnp.maximum(b - 1, 0)]

    @pl.when((b == 0) | (e != prev))
    def _():
        wg_s[...] = wg_ref[0, 0].astype(BF16)
        wu_s[...] = wu_ref[0, 0].astype(BF16)
        wd_s[...] = wd_ref[0, 0].astype(BF16)

    @pl.when(b < nu_ref[0])
    def _():
        x = xs_ref[...].astype(BF16)
        g = _dot(x, wg_s[...]) + bg_ref[0, 0]
        u = _dot(x, wu_s[...]) + bu_ref[0, 0]
        g = jnp.minimum(g, SWIGLU_LIMIT)
        u = jnp.clip(u, -SWIGLU_LIMIT, SWIGLU_LIMIT)
        a = g * jax.nn.sigmoid(SWIGLU_ALPHA * g) * (u + 1.0)
        ys_ref[...] = _dot(a.astype(BF16), wd_s[...]) + bd_ref[0, 0]

    @pl.when(b >= nu_ref[0])
    def _():
        ys_ref[...] = jnp.zeros_like(ys_ref)


def _experts(block_expert, n_used, xs, l, w_gate, b_gate, w_up, b_up, w_down, b_down):
    tm = TM_EXPERT
    n_blocks = xs.shape[0] // tm
    w_spec = pl.BlockSpec((1, 1, D_MODEL, D_MODEL), lambda b, be, nu: (l, be[b], 0, 0))
    b_spec = pl.BlockSpec((1, 1, 1, D_MODEL), lambda b, be, nu: (l, be[b], 0, 0))
    grid_spec = pltpu.PrefetchScalarGridSpec(
        num_scalar_prefetch=2,
        grid=(n_blocks,),
        in_specs=[
            pl.BlockSpec((tm, D_MODEL), lambda b, be, nu: (jnp.minimum(b, nu[0] - 1), 0)),
            w_spec, b_spec, w_spec, b_spec, w_spec, b_spec,
        ],
        out_specs=pl.BlockSpec((tm, D_MODEL), lambda b, be, nu: (b, 0)),
        scratch_shapes=[pltpu.VMEM((D_MODEL, D_MODEL), BF16)] * 3,
    )
    return pl.pallas_call(
        _expert_kernel,
        grid_spec=grid_spec,
        out_shape=jax.ShapeDtypeStruct(xs.shape, F32),
        compiler_params=_cparams(("arbitrary",)),
        name=f"moe_experts_l{l}",
    )(block_expert, n_used, xs, w_gate, b_gate, w_up, b_up, w_down, b_down)


def _combine_kernel(dest_ref, gt_ref, x_ref, g2_ref, ys_ref, *rest, final):
    if final:
        gout_ref, o_ref, ybuf, sem = rest
    else:
        o_ref, ybuf, sem = rest
    tm = x_ref.shape[0]

    def start(n, carry):
        for k in range(TOP_K):
            _row_copy(ys_ref, dest_ref[k, n], ybuf.at[k], n, sem).start()
        return carry

    lax.fori_loop(0, tm, start, 0)

    def wait(n, carry):
        for k in range(TOP_K):
            _row_copy(ys_ref, 0, ybuf.at[k], 0, sem).wait()
        return carry

    lax.fori_loop(0, tm, wait, 0)

    gt = gt_ref[...]
    f = gt[:, 0:1] * ybuf[0]
    for k in range(1, TOP_K):
        f = f + gt[:, k:k + 1] * ybuf[k]
    x = x_ref[...] + g2_ref[0, 0, 0] * f
    if final:
        x = _rmsnorm(x, gout_ref[...])
    o_ref[...] = x


def _combine(dest, gates_t, x, mods, l, ys, norm_out, batch):
    tm = TM_ROUTE
    n_rows = x.shape[0]
    tiles_per_batch = SEQ // tm
    final = norm_out is not None
    row_spec = pl.BlockSpec((tm, D_MODEL), lambda i: (i, 0))
    in_specs = [
        pl.BlockSpec((TOP_K, tm), lambda i: (0, i), memory_space=pltpu.SMEM),
        pl.BlockSpec((tm, TOP_K), lambda i: (i, 0)),
        row_spec,
        _mod_spec(l, 5, tiles_per_batch, batch),
        pl.BlockSpec(memory_space=pl.ANY),
    ]
    args = [dest, gates_t, x, mods, ys]
    if final:
        in_specs.append(pl.BlockSpec((1, D_MODEL), lambda i: (0, 0)))
        args.append(norm_out)
    return pl.pallas_call(
        functools.partial(_combine_kernel, final=final),
        grid=(n_rows // tm,),
        in_specs=in_specs,
        out_specs=row_spec,
        out_shape=jax.ShapeDtypeStruct((n_rows, D_MODEL), F32),
        scratch_shapes=[pltpu.VMEM((TOP_K, tm, D_MODEL), F32), pltpu.SemaphoreType.DMA],
        compiler_params=_cparams(("arbitrary",)),
        name=f"moe_combine_l{l}",
    )(*args)


def _moe(h, idx, gates, rank, counts, x, mods, l, moe_params, norm_out, batch):
    n_rows = h.shape[0]
    tm = TM_EXPERT
    n_blocks = -(-n_rows * TOP_K // tm) + N_EXPERTS
    cnt = counts[:, 0].astype(I32)
    padded = (cnt + tm - 1) // tm * tm
    pad_end = jnp.cumsum(padded)
    pad_start = pad_end - padded
    experts = jnp.arange(N_EXPERTS, dtype=I32)[:, None, None]
    dest = rank + jnp.sum(jnp.where(idx[None] == experts, pad_start[:, None, None], 0), axis=0)
    n_used = pad_end[-1] // tm
    blk = jnp.arange(n_blocks, dtype=I32)
    be = jnp.sum((pad_end[None, :] <= (blk * tm)[:, None]).astype(I32), axis=1)
    last = jnp.sum((pad_end <= (n_used - 1) * tm).astype(I32))
    be = jnp.minimum(jnp.where(blk < n_used, be, last), N_EXPERTS - 1)
    xs = _dispatch(dest, h, n_blocks * tm)
    ys = _experts(be, n_used.reshape(1).astype(I32), xs, l, *moe_params)
    return _combine(dest, gates.T, x, mods, l, ys, norm_out, batch)


def _rope_tables(rot_dim):
    t = jnp.arange(SEQ)
    row = (t // GRID_W).astype(F32)
    col = (t % GRID_W).astype(F32)
    n_freq = rot_dim // 4
    inv = ROPE_THETA ** (-jnp.arange(n_freq, dtype=F32) / n_freq)
    ang = jnp.concatenate([row[:, None] * inv, col[:, None] * inv], axis=-1)
    cos = jnp.repeat(jnp.cos(ang), 2, axis=-1)
    sin = jnp.repeat(jnp.sin(ang), 2, axis=-1) * jnp.tile(jnp.array([-1.0, 1.0], F32), rot_dim // 2)
    reps = LANES // rot_dim
    cos = jnp.concatenate([jnp.tile(cos, (1, reps)), jnp.ones((TM_PROJ, LANES), F32)], axis=0)
    sin = jnp.concatenate([jnp.tile(sin, (1, reps)), jnp.zeros((TM_PROJ, LANES), F32)], axis=0)
    return cos, sin


def _gqa_weights(w_in):
    qd = A_HEADS * A_HEAD_DIM
    kd = A_KV_HEADS * A_HEAD_DIM
    wq = w_in[:, :qd]
    wk = w_in[:, qd:qd + kd].reshape(D_MODEL, A_KV_HEADS, 1, A_HEAD_DIM)
    wv = w_in[:, qd + kd:].reshape(D_MODEL, A_KV_HEADS, 1, A_HEAD_DIM)
    dup = lambda w: jnp.broadcast_to(w, (D_MODEL, A_KV_HEADS, 2, A_HEAD_DIM)).reshape(D_MODEL, 2 * kd)
    return jnp.concatenate([wq, dup(wk), dup(wv)], axis=1).astype(BF16)


def _mla_weights(w_in, w_uq, w_ukv):
    w_in_p = jnp.pad(w_in, ((0, 0), (0, 4 * LANES - w_in.shape[1]))).astype(BF16)
    uq = w_uq.reshape(C_Q_LORA, C_HEADS, C_NOPE + C_ROPE)
    uq = jnp.concatenate([uq[:, :, :C_NOPE].reshape(C_Q_LORA, -1), uq[:, :, C_NOPE:].reshape(C_Q_LORA, -1)], axis=1)
    ukv = w_ukv.reshape(C_KV_LORA, C_HEADS, C_NOPE + C_V)
    ukv = jnp.concatenate([ukv[:, :, :C_NOPE].reshape(C_KV_LORA, -1), ukv[:, :, C_NOPE:].reshape(C_KV_LORA, -1)], axis=1)
    return w_in_p, uq.astype(BF16), ukv.astype(BF16)


def _na_bias(rpb):
    col = np.arange(GRID_W)
    wstart = np.clip(col - NA_COLS // 2, 0, GRID_W - NA_COLS)
    ok_c = (col[None, :] >= wstart[:, None]) & (col[None, :] < wstart[:, None] + NA_COLS)
    dc = col[None, :] - col[:, None] + NA_COLS - 1
    sel_c = (ok_c[:, :, None] & (dc[:, :, None] == np.arange(2 * NA_COLS - 1))).astype(np.float32)
    sel_r, ok_r = [], []
    for r_base, s_row in ((0, 0), (NA_TILE_ROWS, 0), (GRID_H - NA_TILE_ROWS, GRID_H - NA_SLAB_ROWS)):
        r = (r_base + np.arange(NA_TILE_ROWS))[:, None]
        rk = (s_row + np.arange(NA_SLAB_ROWS))[None, :]
        r0 = np.clip(r - NA_ROWS // 2, 0, GRID_H - NA_ROWS)
        ok = (rk >= r0) & (rk < r0 + NA_ROWS)
        dr = rk - r + NA_ROWS - 1
        sel_r.append((ok[:, :, None] & (dr[:, :, None] == np.arange(2 * NA_ROWS - 1))).astype(np.float32))
        ok_r.append(ok)
    sel_r = np.stack(sel_r)
    ok = np.stack(ok_r)[:, :, None, :, None] & ok_c[None, None, :, None, :]
    hi = lax.Precision.HIGHEST
    t = jnp.einsum("hdx,ckx->hdck", rpb.astype(F32), jnp.asarray(sel_c), precision=hi)
    b = jnp.einsum("naid,hdck->hnacik", jnp.asarray(sel_r), t, precision=hi)
    b = jnp.where(jnp.asarray(ok)[None], b, NEG_INF)
    return b.reshape(B_HEADS, 3, NA_TILE_ROWS * GRID_W, NA_SLAB_ROWS * GRID_W)


def kernel(x, c, ctx, c_ctx, ada_w, ada_b, norm_mix, norm_ffn, norm_out, a_w_in, a_w_out, a_sink, b_w_in, b_w_out, b_rpb, c_w_in, c_q_norm, c_kv_norm, c_w_uq, c_w_ukv, c_w_out, router_w, router_b, moe_w_gate, moe_b_gate, moe_w_up, moe_b_up, moe_w_down, moe_b_down):
    batch = x.shape[0]
    assert x.shape[1:] == (SEQ, D_MODEL) and ctx.shape[1:] == (CTX_LEN, D_MODEL)
    assert batch + 1 <= MOD_ROWS
    n_lat = batch * SEQ
    n_ctx = batch * CTX_LEN

    cc = jnp.concatenate([c, c_ctx[None, :], jnp.zeros((MOD_ROWS - batch - 1, D_MODEL), F32)], axis=0)
    mods = _modulation(cc, ada_w, ada_b).reshape(DEPTH, MOD_ROWS, 6, 1, D_MODEL)

    norm_mix3 = norm_mix.reshape(DEPTH, 1, D_MODEL)
    norm_ffn3 = norm_ffn.reshape(DEPTH, 1, D_MODEL)
    router_wt = jnp.swapaxes(router_w, 1, 2)
    router_b3 = router_b.reshape(DEPTH, N_EXPERTS, 1)
    moe_biases = [b.reshape(DEPTH, N_EXPERTS, 1, D_MODEL) for b in (moe_b_gate, moe_b_up, moe_b_down)]
    moe_params = (moe_w_gate, moe_biases[0], moe_w_up, moe_biases[1], moe_w_down, moe_biases[2])
    rope_a = _rope_tables(A_HEAD_DIM)
    rope_c = _rope_tables(C_ROPE)

    xs = jnp.concatenate([x.reshape(n_lat, D_MODEL), ctx.reshape(n_ctx, D_MODEL)], axis=0)
    for l in range(DEPTH):
        last = l == DEPTH - 1
        kind, slot = l % N_MIXERS, l // N_MIXERS
        if kind == 0:
            y = _project(xs, norm_mix3, mods, l, _gqa_weights(a_w_in[slot]), rope_a, N_PAIRS + A_KV_HEADS, batch)
            o = _gqa_attention(y, a_sink[slot], batch)
            oc = None if last else _gqa_ctx_attention(y, a_sink[slot], batch)
            w_out = a_w_out[slot]
        elif kind == 1:
            y = _project(xs, norm_mix3, mods, l, b_w_in[slot].astype(BF16), None, 0, batch)
            o = _na_attention(y, _na_bias(b_rpb[slot]), batch)
            oc = None if last else _na_ctx_attention(y, batch)
            w_out = b_w_out[slot]
        else:
            w_in_p, uq, ukv = _mla_weights(c_w_in[slot], c_w_uq[slot], c_w_ukv[slot])
            y = _mla_project(xs, norm_mix3, mods, l, w_in_p, c_q_norm[slot][None, :], c_kv_norm[slot][None, :],
                             uq, ukv, rope_c, batch)
            o = _mla_attention(y, batch)
            oc = None if last else _mla_ctx_attention(y, batch)
            w_out = c_w_out[slot]
        if not last:
            o = jnp.concatenate([o, oc], axis=1)
        n_rows = o.shape[1]
        x_mid, h, idx, gates, rank, counts = _post_attention(
            o, w_out.astype(BF16), xs, mods, l, norm_ffn3, router_wt, router_b3, n_rows, batch)
        xs = _moe(h, idx, gates, rank, counts, x_mid, mods, l, moe_params,
                  norm_out[None, :] if last else None, batch)
    return xs.reshape(batch, SEQ, D_MODEL)
```

```python
import functools

import numpy as np
import jax
import jax.numpy as jnp
from jax import lax
from jax.experimental import pallas as pl
from jax.experimental.pallas import tpu as pltpu

F32 = jnp.float32
BF16 = jnp.bfloat16
I32 = jnp.int32

D_MODEL = 1024
DEPTH = 4
SEQ = 2048
GRID_W = 64
GRID_H = SEQ // GRID_W
CTX_LEN = 256
N_MIXERS = 3
NORM_EPS = 1e-6
ROPE_THETA = 10000.0
NEG_INF = -1e30

A_HEADS, A_KV_HEADS, A_HEAD_DIM, A_WINDOW = 16, 4, 64, 128
B_HEADS, B_HEAD_DIM, NA_ROWS, NA_COLS = 16, 64, 8, 16
C_HEADS, C_NOPE, C_ROPE, C_V, C_Q_LORA, C_KV_LORA = 16, 64, 32, 64, 256, 128
N_EXPERTS, TOP_K = 32, 4
SWIGLU_LIMIT, SWIGLU_ALPHA = 7.0, 1.702

LANES = 128
N_PAIRS = 8
MOD_ROWS = 16

TM_PROJ = 512
TM_POST = 256
TM_EXPERT = 512
CHUNK = 8
N_SORTED = TOP_K * TM_POST + N_EXPERTS * CHUNK
TQ_GQA = 256
TQ_MLA = 256
NA_TILE_ROWS = 4
NA_SLAB_ROWS = 12
VMEM_MB = 56


def _cparams(sem, vmem_mb=VMEM_MB):
    return pltpu.CompilerParams(dimension_semantics=sem, vmem_limit_bytes=vmem_mb * 1024 * 1024)


def _dot(a, b):
    return jnp.dot(a, b, preferred_element_type=F32)


def _dot_nt(a, b):
    return lax.dot_general(a, b, (((1,), (1,)), ((), ())), preferred_element_type=F32)


def _split(x):
    hi = x.astype(BF16)
    lo = (x - hi.astype(F32)).astype(BF16)
    return hi, lo


def _norm_mod(x, g, shift, scale):
    ms = jnp.mean(x * x, axis=-1, keepdims=True)
    y = x * lax.rsqrt(ms + NORM_EPS) * g
    return y * (1.0 + scale) + shift


def _rmsnorm(x, g):
    ms = jnp.mean(x * x, axis=-1, keepdims=True)
    return x * lax.rsqrt(ms + NORM_EPS) * g


def _rope_block(xb, cos, sin_signed, even):
    nxt = pltpu.roll(xb, LANES - 1, 1)
    prv = pltpu.roll(xb, 1, 1)
    return xb * cos + jnp.where(even, nxt, prv) * sin_signed


def _softmax_pv(s_parts, v_parts, sink):
    m = s_parts[0].max(axis=-1, keepdims=True)
    for s in s_parts[1:]:
        m = jnp.maximum(m, s.max(axis=-1, keepdims=True))
    if sink is not None:
        m = jnp.maximum(m, sink)
    den = None
    acc = None
    for s, v in zip(s_parts, v_parts):
        e = jnp.exp(s - m)
        d = e.sum(axis=-1, keepdims=True)
        a = _dot(e.astype(BF16), v)
        den = d if den is None else den + d
        acc = a if acc is None else acc + a
    if sink is not None:
        den = den + jnp.exp(sink - m)
    return acc / den


def _two_heads(qs, parts, scale, sinks, lo):
    outs = []
    if float(np.log2(scale)).is_integer():
        qs = [q * jnp.asarray(scale, q.dtype) for q in qs]
        scale = 1.0
    for half in range(2):
        s_list = []
        for k, _, post in parts:
            s = _dot_nt(qs[half], k)
            if scale != 1.0:
                s = s * scale
            if post is not None:
                s = post(s, half)
            s_list.append(s)
        sink = None if sinks is None else sinks[half]
        outs.append(_softmax_pv(s_list, [p[1] for p in parts], sink))
    return jnp.where(lo, outs[0], outs[1]).astype(BF16)


def _lane_masks(rows):
    lane = lax.broadcasted_iota(I32, (rows, LANES), 1)
    return lane, lane < (LANES // 2)


def _mask_halves(q, lo):
    z = jnp.zeros_like(q)
    return [jnp.where(lo, q, z), jnp.where(lo, z, q)]


def _mod_kernel(c_ref, w_ref, b_ref, o_ref):
    c = c_ref[...]
    s = c * jax.nn.sigmoid(c)
    s_hi, s_lo = _split(s)
    w_hi, w_lo = _split(w_ref[0])
    o_ref[0] = _dot(s_hi, w_hi) + _dot(s_lo, w_hi) + _dot(s_hi, w_lo) + b_ref[0]


def _modulation(cc, ada_w, ada_b):
    tn = 1536
    n6 = ada_w.shape[-1]
    return pl.pallas_call(
        _mod_kernel,
        grid=(DEPTH, n6 // tn),
        in_specs=[
            pl.BlockSpec((MOD_ROWS, D_MODEL), lambda l, j: (0, 0)),
            pl.BlockSpec((1, D_MODEL, tn), lambda l, j: (l, 0, j)),
            pl.BlockSpec((1, 1, tn), lambda l, j: (l, 0, j)),
        ],
        out_specs=pl.BlockSpec((1, MOD_ROWS, tn), lambda l, j: (l, 0, j)),
        out_shape=jax.ShapeDtypeStruct((DEPTH, MOD_ROWS, n6), F32),
        compiler_params=_cparams(("arbitrary", "arbitrary")),
        name="modulation",
    )(cc, ada_w, ada_b.reshape(DEPTH, 1, n6))


def _mod_spec(l, which, n_lat_tiles_per_batch, batch):
    def index(i):
        return (l, jnp.minimum(i // n_lat_tiles_per_batch, batch), which, 0, 0)
    return pl.BlockSpec((1, 1, 1, 1, D_MODEL), index)


def _proj_kernel(x_ref, g_ref, sh_ref, sc_ref, w_ref, *rest, n_rope):
    if n_rope:
        cos_ref, sin_ref, o_ref = rest
    else:
        (o_ref,) = rest
    h = _norm_mod(x_ref[...], g_ref[0], sh_ref[0, 0, 0], sc_ref[0, 0, 0])
    y = _dot(h.astype(BF16), w_ref[...])
    if n_rope:
        cos = cos_ref[...]
        sin = sin_ref[...]
        lane, _ = _lane_masks(x_ref.shape[0])
        even = (lane & 1) == 0
    for j in range(o_ref.shape[0]):
        yb = y[:, j * LANES:(j + 1) * LANES]
        if j < n_rope:
            yb = _rope_block(yb, cos, sin, even)
        o_ref[j] = yb.astype(BF16)


def _project(x, norm_g, mods, l, w, rope, n_rope, batch):
    n_rows = x.shape[0]
    tm = TM_PROJ
    n_tiles = n_rows // tm
    tiles_per_batch = SEQ // tm
    n_lat_tiles = batch * tiles_per_batch
    ncb = w.shape[1] // LANES
    in_specs = [
        pl.BlockSpec((tm, D_MODEL), lambda i: (i, 0)),
        pl.BlockSpec((1, 1, D_MODEL), lambda i: (l, 0, 0)),
        _mod_spec(l, 0, tiles_per_batch, batch),
        _mod_spec(l, 1, tiles_per_batch, batch),
        pl.BlockSpec(w.shape, lambda i: (0, 0)),
    ]
    args = [x, norm_g, mods, mods, w]
    if n_rope:
        def rope_index(i):
            return (jnp.where(i < n_lat_tiles, i % tiles_per_batch, tiles_per_batch), 0)
        in_specs += [pl.BlockSpec((tm, LANES), rope_index)] * 2
        args += list(rope)
    return pl.pallas_call(
        functools.partial(_proj_kernel, n_rope=n_rope),
        grid=(n_tiles,),
        in_specs=in_specs,
        out_specs=pl.BlockSpec((ncb, tm, LANES), lambda i: (0, i, 0)),
        out_shape=jax.ShapeDtypeStruct((ncb, n_rows, LANES), BF16),
        compiler_params=_cparams(("arbitrary",)),
        name=f"project_l{l}",
    )(*args)


def _mla_proj_kernel(x_ref, g_ref, sh_ref, sc_ref, w_ref, qn_ref, kvn_ref, wq_ref, wkv_ref,
                     cos_ref, sin_ref, o_ref):
    h = _norm_mod(x_ref[...], g_ref[0], sh_ref[0, 0, 0], sc_ref[0, 0, 0])
    y = _dot(h.astype(BF16), w_ref[...])
    cq = _rmsnorm(y[:, :C_Q_LORA], qn_ref[...])
    qq = _dot(cq.astype(BF16), wq_ref[...])
    ckv = _rmsnorm(y[:, C_Q_LORA:C_Q_LORA + C_KV_LORA], kvn_ref[...])
    kv = _dot(ckv.astype(BF16), wkv_ref[...])
    kr = y[:, C_Q_LORA + C_KV_LORA:]
    kr = kr + pltpu.roll(kr, 32, 1) + pltpu.roll(kr, 64, 1) + pltpu.roll(kr, 96, 1)
    cos = cos_ref[...]
    sin = sin_ref[...]
    lane, _ = _lane_masks(x_ref.shape[0])
    even = (lane & 1) == 0
    for j in range(N_PAIRS):
        o_ref[j] = qq[:, j * LANES:(j + 1) * LANES].astype(BF16)
    for j in range(2 * N_PAIRS):
        o_ref[N_PAIRS + j] = kv[:, j * LANES:(j + 1) * LANES].astype(BF16)
    for j in range(4):
        qb = qq[:, (N_PAIRS + j) * LANES:(N_PAIRS + j + 1) * LANES]
        o_ref[3 * N_PAIRS + j] = _rope_block(qb, cos, sin, even).astype(BF16)
    o_ref[3 * N_PAIRS + 4] = _rope_block(kr, cos, sin, even).astype(BF16)


def _mla_project(x, norm_g, mods, l, w_in, q_norm, kv_norm, w_uq, w_ukv, rope, batch):
    n_rows = x.shape[0]
    tm = TM_PROJ
    tiles_per_batch = SEQ // tm
    n_lat_tiles = batch * tiles_per_batch
    ncb = 3 * N_PAIRS + 5

    def rope_index(i):
        return (jnp.where(i < n_lat_tiles, i % tiles_per_batch, tiles_per_batch), 0)

    def full(a):
        return pl.BlockSpec(a.shape, lambda i: (0,) * a.ndim)

    return pl.pallas_call(
        _mla_proj_kernel,
        grid=(n_rows // tm,),
        in_specs=[
            pl.BlockSpec((tm, D_MODEL), lambda i: (i, 0)),
            pl.BlockSpec((1, 1, D_MODEL), lambda i: (l, 0, 0)),
            _mod_spec(l, 0, tiles_per_batch, batch),
            _mod_spec(l, 1, tiles_per_batch, batch),
            full(w_in), full(q_norm), full(kv_norm), full(w_uq), full(w_ukv),
            pl.BlockSpec((tm, LANES), rope_index),
            pl.BlockSpec((tm, LANES), rope_index),
        ],
        out_specs=pl.BlockSpec((ncb, tm, LANES), lambda i: (0, i, 0)),
        out_shape=jax.ShapeDtypeStruct((ncb, n_rows, LANES), BF16),
        compiler_params=_cparams(("arbitrary",)),
        name=f"mla_project_l{l}",
    )(x, norm_g, mods, mods, w_in, q_norm, kv_norm, w_uq, w_ukv, *rope)


def _gqa_kernel(sink_ref, q_ref, k_ref, v_ref, kc_ref, vc_ref, o_ref, *, tq):
    j = pl.program_id(1)
    ks = tq + 2 * A_WINDOW
    s0 = pl.multiple_of(jnp.clip(j * tq - A_WINDOW, 0, SEQ - ks), LANES)
    qpos = j * tq + lax.broadcasted_iota(I32, (tq, ks), 0)
    kpos = s0 + lax.broadcasted_iota(I32, (tq, ks), 1)
    band = jnp.abs(kpos - qpos) <= A_WINDOW
    _, lo = _lane_masks(tq)
    scale = A_HEAD_DIM ** -0.5

    def masked(s, half):
        return jnp.where(band, s, NEG_INF)

    def pair(c, carry):
        kh = c // 2
        k = k_ref[kh, pl.ds(s0, ks), :]
        v = v_ref[kh, pl.ds(s0, ks), :]
        parts = [(k, v, masked), (kc_ref[kh], vc_ref[kh], None)]
        sinks = [sink_ref[2 * c], sink_ref[2 * c + 1]]
        o_ref[c] = _two_heads(_mask_halves(q_ref[c], lo), parts, scale, sinks, lo)
        return carry

    lax.fori_loop(0, N_PAIRS, pair, 0)


def _gqa_attention(y, sink, batch):
    tq = TQ_GQA
    n_lat = batch * SEQ
    ctx0 = n_lat // CTX_LEN
    kvb = A_KV_HEADS
    return pl.pallas_call(
        functools.partial(_gqa_kernel, tq=tq),
        grid=(batch, SEQ // tq),
        in_specs=[
            pl.BlockSpec(memory_space=pltpu.SMEM),
            pl.BlockSpec((N_PAIRS, tq, LANES), lambda b, j: (0, b * (SEQ // tq) + j, 0)),
            pl.BlockSpec((kvb, SEQ, LANES), lambda b, j: (2, b, 0)),
            pl.BlockSpec((kvb, SEQ, LANES), lambda b, j: (3, b, 0)),
            pl.BlockSpec((kvb, CTX_LEN, LANES), lambda b, j: (2, ctx0 + b, 0)),
            pl.BlockSpec((kvb, CTX_LEN, LANES), lambda b, j: (3, ctx0 + b, 0)),
        ],
        out_specs=pl.BlockSpec((N_PAIRS, tq, LANES), lambda b, j: (0, b * (SEQ // tq) + j, 0)),
        out_shape=jax.ShapeDtypeStruct((N_PAIRS, n_lat, LANES), BF16),
        compiler_params=_cparams(("arbitrary", "arbitrary")),
        name="gqa_attention",
    )(sink, y, y, y, y, y)


def _gqa_ctx_kernel(sink_ref, q_ref, k_ref, v_ref, o_ref):
    _, lo = _lane_masks(CTX_LEN)

    def pair(c, carry):
        kh = c // 2
        sinks = [sink_ref[2 * c], sink_ref[2 * c + 1]]
        o_ref[c] = _two_heads(_mask_halves(q_ref[c], lo), [(k_ref[kh], v_ref[kh], None)],
                              A_HEAD_DIM ** -0.5, sinks, lo)
        return carry

    lax.fori_loop(0, N_PAIRS, pair, 0)


def _gqa_ctx_attention(y, sink, batch):
    ctx0 = batch * SEQ // CTX_LEN
    kvb = A_KV_HEADS
    return pl.pallas_call(
        _gqa_ctx_kernel,
        grid=(batch,),
        in_specs=[
            pl.BlockSpec(memory_space=pltpu.SMEM),
            pl.BlockSpec((N_PAIRS, CTX_LEN, LANES), lambda b: (0, ctx0 + b, 0)),
            pl.BlockSpec((kvb, CTX_LEN, LANES), lambda b: (2, ctx0 + b, 0)),
            pl.BlockSpec((kvb, CTX_LEN, LANES), lambda b: (3, ctx0 + b, 0)),
        ],
        out_specs=pl.BlockSpec((N_PAIRS, CTX_LEN, LANES), lambda b: (0, b, 0)),
        out_shape=jax.ShapeDtypeStruct((N_PAIRS, batch * CTX_LEN, LANES), BF16),
        compiler_params=_cparams(("arbitrary",)),
        name="gqa_ctx_attention",
    )(sink, y, y, y)


def _na_kernel(q_ref, k_ref, v_ref, kc_ref, vc_ref, bias_ref, o_ref, *, tq, ks):
    t = pl.program_id(1)
    s0row = jnp.clip(NA_TILE_ROWS * t - NA_ROWS // 2, 0, GRID_H - NA_SLAB_ROWS)
    s0 = pl.multiple_of(s0row * GRID_W, GRID_W)
    _, lo = _lane_masks(tq)

    def pair(c, carry):
        k = k_ref[c, pl.ds(s0, ks), :]
        v = v_ref[c, pl.ds(s0, ks), :]

        def biased(s, half):
            return s + bias_ref[2 * c + half, 0]

        parts = [(k, v, biased), (kc_ref[c], vc_ref[c], None)]
        o_ref[c] = _two_heads(_mask_halves(q_ref[c], lo), parts, B_HEAD_DIM ** -0.5, None, lo)
        return carry

    lax.fori_loop(0, N_PAIRS, pair, 0)


def _na_attention(y, bias, batch):
    tq = NA_TILE_ROWS * GRID_W
    ks = NA_SLAB_ROWS * GRID_W
    n_t = SEQ // tq
    n_lat = batch * SEQ
    ctx0 = n_lat // CTX_LEN

    def bias_index(b, t):
        return (0, jnp.where(t == 0, 0, jnp.where(t == n_t - 1, 2, 1)), 0, 0)

    return pl.pallas_call(
        functools.partial(_na_kernel, tq=tq, ks=ks),
        grid=(batch, n_t),
        in_specs=[
            pl.BlockSpec((N_PAIRS, tq, LANES), lambda b, t: (0, b * n_t + t, 0)),
            pl.BlockSpec((N_PAIRS, SEQ, LANES), lambda b, t: (1, b, 0)),
            pl.BlockSpec((N_PAIRS, SEQ, LANES), lambda b, t: (2, b, 0)),
            pl.BlockSpec((N_PAIRS, CTX_LEN, LANES), lambda b, t: (1, ctx0 + b, 0)),
            pl.BlockSpec((N_PAIRS, CTX_LEN, LANES), lambda b, t: (2, ctx0 + b, 0)),
            pl.BlockSpec((B_HEADS, 1, tq, ks), bias_index),
        ],
        out_specs=pl.BlockSpec((N_PAIRS, tq, LANES), lambda b, t: (0, b * n_t + t, 0)),
        out_shape=jax.ShapeDtypeStruct((N_PAIRS, n_lat, LANES), BF16),
        compiler_params=_cparams(("arbitrary", "arbitrary"), 60),
        name="na_attention",
    )(y, y, y, y, y, bias)


def _na_ctx_kernel(q_ref, k_ref, v_ref, o_ref):
    _, lo = _lane_masks(CTX_LEN)

    def pair(c, carry):
        o_ref[c] = _two_heads(_mask_halves(q_ref[c], lo), [(k_ref[c], v_ref[c], None)],
                              B_HEAD_DIM ** -0.5, None, lo)
        return carry

    lax.fori_loop(0, N_PAIRS, pair, 0)


def _na_ctx_attention(y, batch):
    ctx0 = batch * SEQ // CTX_LEN
    return pl.pallas_call(
        _na_ctx_kernel,
        grid=(batch,),
        in_specs=[
            pl.BlockSpec((N_PAIRS, CTX_LEN, LANES), lambda b: (0, ctx0 + b, 0)),
            pl.BlockSpec((N_PAIRS, CTX_LEN, LANES), lambda b: (1, ctx0 + b, 0)),
            pl.BlockSpec((N_PAIRS, CTX_LEN, LANES), lambda b: (2, ctx0 + b, 0)),
        ],
        out_specs=pl.BlockSpec((N_PAIRS, CTX_LEN, LANES), lambda b: (0, b, 0)),
        out_shape=jax.ShapeDtypeStruct((N_PAIRS, batch * CTX_LEN, LANES), BF16),
        compiler_params=_cparams(("arbitrary",)),
        name="na_ctx_attention",
    )(y, y, y)


def _mla_queries(qn, qr, c, lane, lo):
    zero = jnp.zeros_like(qn)
    quarter = lane // C_ROPE
    qs = []
    for half in range(2):
        u = 2 * (c % 2) + half
        qn_m = jnp.where(lo, qn, zero) if half == 0 else jnp.where(lo, zero, qn)
        qr_m = jnp.where(quarter == u, qr, zero)
        qs.append(jnp.concatenate([qn_m, qr_m], axis=1))
    return qs


def _mla_kernel(qn_ref, qr_ref, kn_ref, v_ref, kr_ref, knc_ref, vc_ref, krc_ref, o_ref, *, tq):
    lane, lo = _lane_masks(tq)
    scale = (C_NOPE + C_ROPE) ** -0.5

    def pair(c, carry):
        kcat = jnp.concatenate([kn_ref[c], kr_ref[0]], axis=1)
        kccat = jnp.concatenate([knc_ref[c], krc_ref[0]], axis=1)
        parts = [(kccat, vc_ref[c], None), (kcat, v_ref[c], None)]
        qs = _mla_queries(qn_ref[c], qr_ref[c // 2], c, lane, lo)
        o_ref[c] = _two_heads(qs, parts, scale, None, lo)
        return carry

    lax.fori_loop(0, N_PAIRS, pair, 0)


def _mla_attention(y, batch):
    tq = TQ_MLA
    n_t = SEQ // tq
    n_lat = batch * SEQ
    ctx0 = n_lat // CTX_LEN
    kr_blk = 3 * N_PAIRS + 4
    return pl.pallas_call(
        functools.partial(_mla_kernel, tq=tq),
        grid=(batch, n_t),
        in_specs=[
            pl.BlockSpec((N_PAIRS, tq, LANES), lambda b, t: (0, b * n_t + t, 0)),
            pl.BlockSpec((4, tq, LANES), lambda b, t: (6, b * n_t + t, 0)),
            pl.BlockSpec((N_PAIRS, SEQ, LANES), lambda b, t: (1, b, 0)),
            pl.BlockSpec((N_PAIRS, SEQ, LANES), lambda b, t: (2, b, 0)),
            pl.BlockSpec((1, SEQ, LANES), lambda b, t: (kr_blk, b, 0)),
            pl.BlockSpec((N_PAIRS, CTX_LEN, LANES), lambda b, t: (1, ctx0 + b, 0)),
            pl.BlockSpec((N_PAIRS, CTX_LEN, LANES), lambda b, t: (2, ctx0 + b, 0)),
            pl.BlockSpec((1, CTX_LEN, LANES), lambda b, t: (kr_blk, ctx0 + b, 0)),
        ],
        out_specs=pl.BlockSpec((N_PAIRS, tq, LANES), lambda b, t: (0, b * n_t + t, 0)),
        out_shape=jax.ShapeDtypeStruct((N_PAIRS, n_lat, LANES), BF16),
        compiler_params=_cparams(("arbitrary", "arbitrary")),
        name="mla_attention",
    )(y, y, y, y, y, y, y, y)


def _mla_ctx_kernel(qn_ref, qr_ref, kn_ref, v_ref, kr_ref, o_ref):
    lane, lo = _lane_masks(CTX_LEN)

    def pair(c, carry):
        kcat = jnp.concatenate([kn_ref[c], kr_ref[0]], axis=1)
        qs = _mla_queries(qn_ref[c], qr_ref[c // 2], c, lane, lo)
        o_ref[c] = _two_heads(qs, [(kcat, v_ref[c], None)], (C_NOPE + C_ROPE) ** -0.5, None, lo)
        return carry

    lax.fori_loop(0, N_PAIRS, pair, 0)


def _mla_ctx_attention(y, batch):
    ctx0 = batch * SEQ // CTX_LEN
    kr_blk = 3 * N_PAIRS + 4
    return pl.pallas_call(
        _mla_ctx_kernel,
        grid=(batch,),
        in_specs=[
            pl.BlockSpec((N_PAIRS, CTX_LEN, LANES), lambda b: (0, ctx0 + b, 0)),
            pl.BlockSpec((4, CTX_LEN, LANES), lambda b: (6, ctx0 + b, 0)),
            pl.BlockSpec((N_PAIRS, CTX_LEN, LANES), lambda b: (1, ctx0 + b, 0)),
            pl.BlockSpec((N_PAIRS, CTX_LEN, LANES), lambda b: (2, ctx0 + b, 0)),
            pl.BlockSpec((1, CTX_LEN, LANES), lambda b: (kr_blk, ctx0 + b, 0)),
        ],
        out_specs=pl.BlockSpec((N_PAIRS, CTX_LEN, LANES), lambda b: (0, b, 0)),
        out_shape=jax.ShapeDtypeStruct((N_PAIRS, batch * CTX_LEN, LANES), BF16),
        compiler_params=_cparams(("arbitrary",)),
        name="mla_ctx_attention",
    )(y, y, y, y, y)


def _post_kernel(o_ref, w_ref, x_ref, g1_ref, gn_ref, sh_ref, sc_ref, rw_ref, rb_ref,
                 xo_ref, h_ref, gate_ref, pos_ref, cnt_ref):
    tm = x_ref.shape[0]
    o = jnp.concatenate([o_ref[c] for c in range(N_PAIRS)], axis=1)
    x = x_ref[...] + g1_ref[0, 0, 0] * _dot(o, w_ref[...])
    xo_ref[...] = x
    h = _norm_mod(x, gn_ref[0], sh_ref[0, 0, 0], sc_ref[0, 0, 0])
    h_ref[...] = h.astype(BF16)

    h_hi, h_lo = _split(h)
    rw_hi, rw_lo = _split(rw_ref[0])
    logits = _dot_nt(rw_hi, h_hi) + _dot_nt(rw_lo, h_hi) + _dot_nt(rw_hi, h_lo) + rb_ref[0]

    ie = lax.broadcasted_iota(I32, (N_EXPERTS, tm), 0).astype(F32)
    work = logits
    vals, sels = [], []
    for k in range(TOP_K):
        m = work.max(axis=0, keepdims=True)
        idx = jnp.min(jnp.where(work == m, ie, float(N_EXPERTS)), axis=0, keepdims=True)
        sel = ie == idx
        vals.append(m)
        sels.append(sel)
        work = jnp.where(sel, -3.0e38, work)
    es = [jnp.exp(v - vals[0]) for v in vals]
    den = es[0] + es[1] + es[2] + es[3]
    for k in range(TOP_K):
        gate_ref[k:k + 1, :] = es[k] / den

    onehot = sels[0].astype(F32) + sels[1].astype(F32) + sels[2].astype(F32) + sels[3].astype(F32)
    before = lax.broadcasted_iota(I32, (tm, tm), 0) < lax.broadcasted_iota(I32, (tm, tm), 1)
    prefix = _dot(onehot.astype(BF16), before.astype(F32).astype(BF16))
    count = jnp.broadcast_to(onehot.sum(axis=1, keepdims=True), (N_EXPERTS, LANES))
    padded = jnp.floor((count + (CHUNK - 1)) * (1.0 / CHUNK)) * CHUNK
    lower = (lax.broadcasted_iota(I32, (N_EXPERTS, N_EXPERTS), 1)
             < lax.broadcasted_iota(I32, (N_EXPERTS, N_EXPERTS), 0))
    offset = _dot(lower.astype(F32).astype(BF16), padded.astype(BF16))[:, 0:1]
    place = prefix + offset
    for k in range(TOP_K):
        pos = jnp.sum(jnp.where(sels[k], place, 0.0), axis=0, keepdims=True)
        pos_ref[k:k + 1, :] = pos.astype(I32)
    cnt_ref[0] = count


def _post_attention(o, w_out, x, mods, l, norm_g, router_wt, router_b, n_rows, batch):
    tm = TM_POST
    n_tiles = n_rows // tm
    tiles_per_batch = SEQ // tm
    row_spec = pl.BlockSpec((tm, D_MODEL), lambda i: (i, 0))
    tok_spec = pl.BlockSpec((TOP_K, tm), lambda i: (0, i))
    return pl.pallas_call(
        _post_kernel,
        grid=(n_tiles,),
        in_specs=[
            pl.BlockSpec((N_PAIRS, tm, LANES), lambda i: (0, i, 0)),
            pl.BlockSpec(w_out.shape, lambda i: (0, 0)),
            row_spec,
            _mod_spec(l, 2, tiles_per_batch, batch),
            pl.BlockSpec((1, 1, D_MODEL), lambda i: (l, 0, 0)),
            _mod_spec(l, 3, tiles_per_batch, batch),
            _mod_spec(l, 4, tiles_per_batch, batch),
            pl.BlockSpec((1, N_EXPERTS, D_MODEL), lambda i: (l, 0, 0)),
            pl.BlockSpec((1, N_EXPERTS, 1), lambda i: (l, 0, 0)),
        ],
        out_specs=[row_spec, row_spec, tok_spec, tok_spec,
                   pl.BlockSpec((1, N_EXPERTS, LANES), lambda i: (i, 0, 0))],
        out_shape=[
            jax.ShapeDtypeStruct((n_rows, D_MODEL), F32),
            jax.ShapeDtypeStruct((n_rows, D_MODEL), BF16),
            jax.ShapeDtypeStruct((TOP_K, n_rows), F32),
            jax.ShapeDtypeStruct((TOP_K, n_rows), I32),
            jax.ShapeDtypeStruct((n_tiles, N_EXPERTS, LANES), F32),
        ],
        compiler_params=_cparams(("arbitrary",)),
        name=f"post_attention_l{l}",
    )(o, w_out, x, mods, norm_g, mods, mods, router_wt, router_b)


def _pack_pairs(y):
    half = y.shape[1] // 2
    hi = pltpu.bitcast(y[:, :half].astype(BF16).astype(F32), jnp.uint32)
    lo = pltpu.bitcast(y[:, half:].astype(BF16).astype(F32), jnp.uint32)
    return hi | (lo >> 16)


def _unpack_pairs(u):
    hi = pltpu.bitcast(u & jnp.uint32(0xFFFF0000), F32)
    lo = pltpu.bitcast(u << 16, F32)
    return hi.astype(BF16), lo.astype(BF16)


def _chunk_copy(src, src_row, dst, dst_row, sem):
    return pltpu.make_async_copy(src.at[pl.ds(src_row, CHUNK), :], dst.at[pl.ds(dst_row, CHUNK), :], sem)


def _for_each_chunk(tile, first_a_ref, first_b_ref, n_chunks_ref, fn):
    def expert(e, carry):
        base = tile * N_EXPERTS + e

        def chunk(j, c):
            fn(pl.multiple_of(first_a_ref[base] + j * CHUNK, CHUNK),
               pl.multiple_of(first_b_ref[base] + j * CHUNK, CHUNK))
            return c

        lax.fori_loop(0, n_chunks_ref[base], chunk, 0)
        return carry

    lax.fori_loop(0, N_EXPERTS, expert, 0)


def _dispatch_kernel(slot_ref, off_ref, nch_ref, zrow_ref, znch_ref, nu_ref,
                     pos_ref, h_ref, xs_ref, sbuf, zbuf, sem):
    i = pl.program_id(0)
    tm = h_ref.shape[0]

    @pl.when(i == 0)
    def _():
        zbuf[...] = jnp.zeros_like(zbuf)

    p_iota = lax.broadcasted_iota(I32, (N_SORTED, tm), 0)
    hit = p_iota == pos_ref[0:1, :]
    for k in range(1, TOP_K):
        hit = hit | (p_iota == pos_ref[k:k + 1, :])
    perm = jnp.where(hit, 1.0, 0.0).astype(BF16)
    sbuf[...] = _pack_pairs(_dot(perm, h_ref[...]))

    _for_each_chunk(i, off_ref, slot_ref, nch_ref,
                    lambda s, d: _chunk_copy(sbuf, s, xs_ref, d, sem).start())
    _for_each_chunk(i, off_ref, slot_ref, nch_ref,
                    lambda s, d: _chunk_copy(sbuf, 0, xs_ref, 0, sem).wait())

    @pl.when(i == pl.num_programs(0) - 1)
    def _():
        def tail(fn):
            def expert(e, carry):
                def chunk(j, c):
                    fn(pl.multiple_of(zrow_ref[e] + j * CHUNK, CHUNK))
                    return c
                lax.fori_loop(0, znch_ref[e], chunk, 0)
                return carry
            lax.fori_loop(0, N_EXPERTS, expert, 0)

        tail(lambda d: _chunk_copy(zbuf, 0, xs_ref, d, sem).start())
        tail(lambda d: _chunk_copy(zbuf, 0, xs_ref, 0, sem).wait())

        def block_copy(b):
            row = pl.multiple_of(b * TM_EXPERT, TM_EXPERT)
            return pltpu.make_async_copy(zbuf, xs_ref.at[pl.ds(row, TM_EXPERT), :], sem)

        def unused(fn):
            def block(b, carry):
                fn(b)
                return carry
            lax.fori_loop(nu_ref[0], xs_ref.shape[0] // TM_EXPERT, block, 0)

        unused(lambda b: block_copy(b).start())
        unused(lambda b: block_copy(0).wait())


def _dispatch(route, pos, h, n_slots):
    tm = TM_POST
    n_rows = h.shape[0]
    half = D_MODEL // 2
    grid_spec = pltpu.PrefetchScalarGridSpec(
        num_scalar_prefetch=6,
        grid=(n_rows // tm,),
        in_specs=[
            pl.BlockSpec((TOP_K, tm), lambda i, *_: (0, i)),
            pl.BlockSpec((tm, D_MODEL), lambda i, *_: (i, 0)),
        ],
        out_specs=pl.BlockSpec(memory_space=pl.ANY),
        scratch_shapes=[pltpu.VMEM((N_SORTED, half), jnp.uint32),
                        pltpu.VMEM((TM_EXPERT, half), jnp.uint32),
                        pltpu.SemaphoreType.DMA],
    )
    return pl.pallas_call(
        _dispatch_kernel,
        grid_spec=grid_spec,
        out_shape=jax.ShapeDtypeStruct((n_slots, half), jnp.uint32),
        compiler_params=_cparams(("arbitrary",)),
        name="moe_dispatch",
    )(route["slot"], route["off"], route["nch"], route["zrow"], route["znch"], route["n_used"], pos, h)


def _expert_kernel(be_ref, nu_ref, xs_ref, wg_ref, bg_ref, wu_ref, bu_ref, wd_ref, bd_ref, ys_ref,
                   wg_s, wu_s, wd_s):
    b = pl.program_id(0)
    e = be_ref[b]
    prev = be_ref[jnp.maximum(b - 1, 0)]

    @pl.when((b == 0) | (e != prev))
    def _():
        wg_s[...] = wg_ref[0, 0].astype(BF16)
        wu_s[...] = wu_ref[0, 0].astype(BF16)
        wd_s[...] = wd_ref[0, 0].astype(BF16)

    @pl.when(b < nu_ref[0])
    def _():
        x = jnp.concatenate(_unpack_pairs(xs_ref[...]), axis=1)
        g = _dot(x, wg_s[...]) + bg_ref[0, 0]
        u = _dot(x, wu_s[...]) + bu_ref[0, 0]
        g = jnp.minimum(g, SWIGLU_LIMIT)
        u = jnp.clip(u, -SWIGLU_LIMIT, SWIGLU_LIMIT)
        a = g * jax.nn.sigmoid(SWIGLU_ALPHA * g) * (u + 1.0)
        ys_ref[...] = _pack_pairs(_dot(a.astype(BF16), wd_s[...]) + bd_ref[0, 0])

    @pl.when(b >= nu_ref[0])
    def _():
        ys_ref[...] = jnp.zeros_like(ys_ref)


def _experts(block_expert, n_used, xs, l, w_gate, b_gate, w_up, b_up, w_down, b_down):
    tm = TM_EXPERT
    n_blocks = xs.shape[0] // tm
    w_spec = pl.BlockSpec((1, 1, D_MODEL, D_MODEL), lambda b, be, nu: (l, be[b], 0, 0))
    b_spec = pl.BlockSpec((1, 1, 1, D_MODEL), lambda b, be, nu: (l, be[b], 0, 0))
    grid_spec = pltpu.PrefetchScalarGridSpec(
        num_scalar_prefetch=2,
        grid=(n_blocks,),
        in_specs=[
            pl.BlockSpec((tm, D_MODEL // 2), lambda b, be, nu: (jnp.minimum(b, nu[0] - 1), 0)),
            w_spec, b_spec, w_spec, b_spec, w_spec, b_spec,
        ],
        out_specs=pl.BlockSpec((tm, D_MODEL // 2), lambda b, be, nu: (b, 0)),
        scratch_shapes=[pltpu.VMEM((D_MODEL, D_MODEL), BF16)] * 3,
    )
    return pl.pallas_call(
        _expert_kernel,
        grid_spec=grid_spec,
        out_shape=jax.ShapeDtypeStruct(xs.shape, jnp.uint32),
        compiler_params=_cparams(("arbitrary",)),
        name=f"moe_experts_l{l}",
    )(block_expert, n_used, xs, w_gate, b_gate, w_up, b_up, w_down, b_down)


def _combine_kernel(slot_ref, off_ref, nch_ref, pos_ref, gt_ref, x_ref, g2_ref, ys_ref, *rest, final):
    if final:
        gout_ref, o_ref, ybuf, sem = rest
    else:
        o_ref, ybuf, sem = rest
    i = pl.program_id(0)
    tm = x_ref.shape[0]

    @pl.when(i == 0)
    def _():
        ybuf[...] = jnp.zeros_like(ybuf)

    _for_each_chunk(i, slot_ref, off_ref, nch_ref,
                    lambda s, d: _chunk_copy(ys_ref, s, ybuf, d, sem).start())
    _for_each_chunk(i, slot_ref, off_ref, nch_ref,
                    lambda s, d: _chunk_copy(ys_ref, 0, ybuf, 0, sem).wait())

    ya, yb = _unpack_pairs(ybuf[...])
    p_iota = lax.broadcasted_iota(I32, (tm, N_SORTED), 1)
    pos = pos_ref[...]
    gt = gt_ref[...]
    wgt = jnp.where(p_iota == pos[:, 0:1], gt[:, 0:1], 0.0)
    for k in range(1, TOP_K):
        wgt = wgt + jnp.where(p_iota == pos[:, k:k + 1], gt[:, k:k + 1], 0.0)
    w_hi, w_lo = _split(wgt)
    f = jnp.concatenate([_dot(w_hi, ya) + _dot(w_lo, ya), _dot(w_hi, yb) + _dot(w_lo, yb)], axis=1)
    x = x_ref[...] + g2_ref[0, 0, 0] * f
    if final:
        x = _rmsnorm(x, gout_ref[...])
    o_ref[...] = x


def _combine(route, pos_t, gates_t, x, mods, l, ys, norm_out, batch):
    tm = TM_POST
    n_rows = pos_t.shape[0]
    tiles_per_batch = SEQ // tm
    final = norm_out is not None
    row_spec = pl.BlockSpec((tm, D_MODEL), lambda i, *_: (i, 0))
    tok_spec = pl.BlockSpec((tm, TOP_K), lambda i, *_: (i, 0))
    mod_spec = _mod_spec(l, 5, tiles_per_batch, batch)
    in_specs = [
        tok_spec, tok_spec, row_spec,
        pl.BlockSpec(mod_spec.block_shape, lambda i, *_: mod_spec.index_map(i)),
        pl.BlockSpec(memory_space=pl.ANY),
    ]
    args = [pos_t, gates_t, x, mods, ys]
    if final:
        in_specs.append(pl.BlockSpec((1, D_MODEL), lambda i, *_: (0, 0)))
        args.append(norm_out)
    grid_spec = pltpu.PrefetchScalarGridSpec(
        num_scalar_prefetch=3,
        grid=(n_rows // tm,),
        in_specs=in_specs,
        out_specs=row_spec,
        scratch_shapes=[pltpu.VMEM((N_SORTED, D_MODEL // 2), jnp.uint32), pltpu.SemaphoreType.DMA],
    )
    return pl.pallas_call(
        functools.partial(_combine_kernel, final=final),
        grid_spec=grid_spec,
        out_shape=jax.ShapeDtypeStruct((n_rows, D_MODEL), F32),
        compiler_params=_cparams(("arbitrary",)),
        name=f"moe_combine_l{l}",
    )(route["slot"], route["off"], route["nch"], *args)


def _moe(h, gates, pos, tile_counts, x, mods, l, moe_params, norm_out, batch):
    n_rows = h.shape[0]
    tmx = TM_EXPERT
    n_tiles = n_rows // TM_POST
    n_blocks = -(-(n_rows * TOP_K + n_tiles * N_EXPERTS * (CHUNK - 1)) // tmx) + N_EXPERTS
    c = tile_counts[:, :, 0].astype(I32)
    c = (c + CHUNK - 1) // CHUNK * CHUNK
    used = c.sum(axis=0)
    region = (used + tmx - 1) // tmx * tmx
    pad_end = jnp.cumsum(region)
    pad_start = pad_end - region
    run = jnp.cumsum(c, axis=0) - c
    off = jnp.cumsum(c, axis=1) - c
    route = {
        "slot": (pad_start[None, :] + run).reshape(-1),
        "off": off.reshape(-1),
        "nch": (c // CHUNK).reshape(-1),
        "zrow": pad_start + used,
        "znch": (region - used) // CHUNK,
        "n_used": (pad_end[-1:] // tmx).astype(I32),
    }
    n_used = route["n_used"][0]
    blk = jnp.arange(n_blocks, dtype=I32)
    be = jnp.sum((pad_end[None, :] <= (blk * tmx)[:, None]).astype(I32), axis=1)
    last = jnp.sum((pad_end <= (n_used - 1) * tmx).astype(I32))
    be = jnp.minimum(jnp.where(blk < n_used, be, last), N_EXPERTS - 1)
    xs = _dispatch(route, pos, h, n_blocks * tmx)
    ys = _experts(be, n_used.reshape(1).astype(I32), xs, l, *moe_params)
    return _combine(route, pos.T, gates.T, x, mods, l, ys, norm_out, batch)


def _rope_tables(rot_dim):
    t = jnp.arange(SEQ)
    row = (t // GRID_W).astype(F32)
    col = (t % GRID_W).astype(F32)
    n_freq = rot_dim // 4
    inv = ROPE_THETA ** (-jnp.arange(n_freq, dtype=F32) / n_freq)
    ang = jnp.concatenate([row[:, None] * inv, col[:, None] * inv], axis=-1)
    cos = jnp.repeat(jnp.cos(ang), 2, axis=-1)
    sin = jnp.repeat(jnp.sin(ang), 2, axis=-1) * jnp.tile(jnp.array([-1.0, 1.0], F32), rot_dim // 2)
    reps = LANES // rot_dim
    cos = jnp.concatenate([jnp.tile(cos, (1, reps)), jnp.ones((TM_PROJ, LANES), F32)], axis=0)
    sin = jnp.concatenate([jnp.tile(sin, (1, reps)), jnp.zeros((TM_PROJ, LANES), F32)], axis=0)
    return cos, sin


def _gqa_weights(w_in):
    qd = A_HEADS * A_HEAD_DIM
    kd = A_KV_HEADS * A_HEAD_DIM
    wq = w_in[:, :qd]
    wk = w_in[:, qd:qd + kd].reshape(D_MODEL, A_KV_HEADS, 1, A_HEAD_DIM)
    wv = w_in[:, qd + kd:].reshape(D_MODEL, A_KV_HEADS, 1, A_HEAD_DIM)
    dup = lambda w: jnp.broadcast_to(w, (D_MODEL, A_KV_HEADS, 2, A_HEAD_DIM)).reshape(D_MODEL, 2 * kd)
    return jnp.concatenate([wq, dup(wk), dup(wv)], axis=1).astype(BF16)


def _mla_weights(w_in, w_uq, w_ukv):
    w_in_p = jnp.pad(w_in, ((0, 0), (0, 4 * LANES - w_in.shape[1]))).astype(BF16)
    uq = w_uq.reshape(C_Q_LORA, C_HEADS, C_NOPE + C_ROPE)
    uq = jnp.concatenate([uq[:, :, :C_NOPE].reshape(C_Q_LORA, -1), uq[:, :, C_NOPE:].reshape(C_Q_LORA, -1)], axis=1)
    ukv = w_ukv.reshape(C_KV_LORA, C_HEADS, C_NOPE + C_V)
    ukv = jnp.concatenate([ukv[:, :, :C_NOPE].reshape(C_KV_LORA, -1), ukv[:, :, C_NOPE:].reshape(C_KV_LORA, -1)], axis=1)
    return w_in_p, uq.astype(BF16), ukv.astype(BF16)


def _na_bias(rpb):
    col = np.arange(GRID_W)
    wstart = np.clip(col - NA_COLS // 2, 0, GRID_W - NA_COLS)
    ok_c = (col[None, :] >= wstart[:, None]) & (col[None, :] < wstart[:, None] + NA_COLS)
    dc = col[None, :] - col[:, None] + NA_COLS - 1
    sel_c = (ok_c[:, :, None] & (dc[:, :, None] == np.arange(2 * NA_COLS - 1))).astype(np.float32)
    sel_r, ok_r = [], []
    for r_base, s_row in ((0, 0), (NA_TILE_ROWS, 0), (GRID_H - NA_TILE_ROWS, GRID_H - NA_SLAB_ROWS)):
        r = (r_base + np.arange(NA_TILE_ROWS))[:, None]
        rk = (s_row + np.arange(NA_SLAB_ROWS))[None, :]
        r0 = np.clip(r - NA_ROWS // 2, 0, GRID_H - NA_ROWS)
        ok = (rk >= r0) & (rk < r0 + NA_ROWS)
        dr = rk - r + NA_ROWS - 1
        sel_r.append((ok[:, :, None] & (dr[:, :, None] == np.arange(2 * NA_ROWS - 1))).astype(np.float32))
        ok_r.append(ok)
    sel_r = np.stack(sel_r)
    ok = np.stack(ok_r)[:, :, None, :, None] & ok_c[None, None, :, None, :]
    hi = lax.Precision.HIGHEST
    t = jnp.einsum("hdx,ckx->hdck", rpb.astype(F32), jnp.asarray(sel_c), precision=hi)
    b = jnp.einsum("naid,hdck->hnacik", jnp.asarray(sel_r), t, precision=hi)
    b = jnp.where(jnp.asarray(ok)[None], b, NEG_INF)
    return b.reshape(B_HEADS, 3, NA_TILE_ROWS * GRID_W, NA_SLAB_ROWS * GRID_W)


def kernel(x, c, ctx, c_ctx, ada_w, ada_b, norm_mix, norm_ffn, norm_out, a_w_in, a_w_out, a_sink, b_w_in, b_w_out, b_rpb, c_w_in, c_q_norm, c_kv_norm, c_w_uq, c_w_ukv, c_w_out, router_w, router_b, moe_w_gate, moe_b_gate, moe_w_up, moe_b_up, moe_w_down, moe_b_down):
    batch = x.shape[0]
    assert x.shape[1:] == (SEQ, D_MODEL) and ctx.shape[1:] == (CTX_LEN, D_MODEL)
    assert batch + 1 <= MOD_ROWS
    n_lat = batch * SEQ
    n_ctx = batch * CTX_LEN

    cc = jnp.concatenate([c, c_ctx[None, :], jnp.zeros((MOD_ROWS - batch - 1, D_MODEL), F32)], axis=0)
    mods = _modulation(cc, ada_w, ada_b).reshape(DEPTH, MOD_ROWS, 6, 1, D_MODEL)

    norm_mix3 = norm_mix.reshape(DEPTH, 1, D_MODEL)
    norm_ffn3 = norm_ffn.reshape(DEPTH, 1, D_MODEL)
    router_wt = jnp.swapaxes(router_w, 1, 2)
    router_b3 = router_b.reshape(DEPTH, N_EXPERTS, 1)
    moe_biases = [b.reshape(DEPTH, N_EXPERTS, 1, D_MODEL) for b in (moe_b_gate, moe_b_up, moe_b_down)]
    moe_params = (moe_w_gate, moe_biases[0], moe_w_up, moe_biases[1], moe_w_down, moe_biases[2])
    rope_a = _rope_tables(A_HEAD_DIM)
    rope_c = _rope_tables(C_ROPE)

    xs = jnp.concatenate([x.reshape(n_lat, D_MODEL), ctx.reshape(n_ctx, D_MODEL)], axis=0)
    for l in range(DEPTH):
        last = l == DEPTH - 1
        kind, slot = l % N_MIXERS, l // N_MIXERS
        if kind == 0:
            y = _project(xs, norm_mix3, mods, l, _gqa_weights(a_w_in[slot]), rope_a, N_PAIRS + A_KV_HEADS, batch)
            o = _gqa_attention(y, a_sink[slot], batch)
            oc = None if last else _gqa_ctx_attention(y, a_sink[slot], batch)
            w_out = a_w_out[slot]
        elif kind == 1:
            y = _project(xs, norm_mix3, mods, l, b_w_in[slot].astype(BF16), None, 0, batch)
            o = _na_attention(y, _na_bias(b_rpb[slot]), batch)
            oc = None if last else _na_ctx_attention(y, batch)
            w_out = b_w_out[slot]
        else:
            w_in_p, uq, ukv = _mla_weights(c_w_in[slot], c_w_uq[slot], c_w_ukv[slot])
            y = _mla_project(xs, norm_mix3, mods, l, w_in_p, c_q_norm[slot][None, :], c_kv_norm[slot][None, :],
                             uq, ukv, rope_c, batch)
            o = _mla_attention(y, batch)
            oc = None if last else _mla_ctx_attention(y, batch)
            w_out = c_w_out[slot]
        if not last:
            o = jnp.concatenate([o, oc], axis=1)
        n_rows = o.shape[1]
        x_mid, h, gates, pos, tile_counts = _post_attention(
            o, w_out.astype(BF16), xs, mods, l, norm_ffn3, router_wt, router_b3, n_rows, batch)
        xs = _moe(h, gates, pos, tile_counts, x_mid, mods, l, moe_params,
                  norm_out[None, :] if last else None, batch)
    return xs.reshape(batch, SEQ, D_MODEL)
```
